```python
import jax, jax.numpy as jnp
from jax import lax
import numpy as np

D_MODEL = 1024
BATCH = 4
SEQ = 4096
DEPTH = 4

N_MIXERS = 4
RMS_EPS = 1e-6
LN_EPS = 1e-5

MLA_HEADS = 16
MLA_Q_RANK = 384
MLA_KV_RANK = 256
MLA_NOPE = 64
MLA_ROPE = 32
MLA_V = 64
ROPE_BASE = 10000.0
Q_BLOCK = 128

LRU_WIDTH = D_MODEL
LRU_HEADS = 8
LRU_HEAD_DIM = LRU_WIDTH // LRU_HEADS
CONV_WIDTH = 4
LRU_C = 8.0

POOL_WINDOWS = (2, 4, 8, 16)
POOL_GROUP = D_MODEL // len(POOL_WINDOWS)

GMLP_WIDTH = D_MODEL
GMLP_GROUPS = 4
GMLP_GROUP_DIM = GMLP_WIDTH // GMLP_GROUPS
GMLP_CHUNK = 128

D_FF = 3584
N_EXPERTS = 8
TOP_K = 2
MOE_BLOCK = 256

kernel_name = "hybrid_interleaved_mla_rglru_pool_gmlp_moe"


def _count(m, period):
    return len(range(m, DEPTH, period))


def rmsnorm(x, g):
    xf = x.astype(jnp.float32)
    y = xf * lax.rsqrt(jnp.mean(xf * xf, axis=-1, keepdims=True) + RMS_EPS)
    return (y * g.astype(jnp.float32)).astype(x.dtype)


def layernorm(x, g, b):
    xf = x.astype(jnp.float32)
    mu = jnp.mean(xf, axis=-1, keepdims=True)
    var = jnp.mean(jnp.square(xf - mu), axis=-1, keepdims=True)
    y = (xf - mu) * lax.rsqrt(var + LN_EPS)
    return (y * g.astype(jnp.float32) + b.astype(jnp.float32)).astype(x.dtype)


def apply_rope(x, cos, sin):
    half = x.shape[-1] // 2
    x1 = x[..., :half].astype(jnp.float32)
    x2 = x[..., half:].astype(jnp.float32)
    return jnp.concatenate([x1 * cos - x2 * sin, x2 * cos + x1 * sin], axis=-1).astype(x.dtype)


def mla_mixer(h, positions, w_in, q_norm, kv_norm, w_uq, w_ukv, w_o):
    B, S, _ = h.shape
    lat = h @ w_in
    c_q, c_kv, k_rope = jnp.split(lat, [MLA_Q_RANK, MLA_Q_RANK + MLA_KV_RANK], axis=-1)
    c_q = rmsnorm(c_q, q_norm)
    c_kv = rmsnorm(c_kv, kv_norm)
    q = (c_q @ w_uq).reshape(B, S, MLA_HEADS, MLA_NOPE + MLA_ROPE)
    q_nope, q_rope = q[..., :MLA_NOPE], q[..., MLA_NOPE:]
    kv = (c_kv @ w_ukv).reshape(B, S, MLA_HEADS, MLA_NOPE + MLA_V)
    k_nope, v = kv[..., :MLA_NOPE], kv[..., MLA_NOPE:]
    half = MLA_ROPE // 2
    inv_freq = ROPE_BASE ** (-jnp.arange(half, dtype=jnp.float32) / half)
    ang = positions.astype(jnp.float32)[..., None] * inv_freq
    cos, sin = jnp.cos(ang), jnp.sin(ang)
    q_rope = apply_rope(q_rope, cos[:, :, None, :], sin[:, :, None, :])
    k_rope = apply_rope(k_rope, cos, sin)
    scale = (MLA_NOPE + MLA_ROPE) ** -0.5
    nb = S // Q_BLOCK
    qn = q_nope.reshape(B, nb, Q_BLOCK, MLA_HEADS, MLA_NOPE).transpose(1, 0, 2, 3, 4)
    qr = q_rope.reshape(B, nb, Q_BLOCK, MLA_HEADS, MLA_ROPE).transpose(1, 0, 2, 3, 4)
    key_pos = jnp.arange(S)

    def block(args):
        qn_b, qr_b, b_idx = args
        s = (jnp.einsum('bqhd,bkhd->bhqk', qn_b, k_nope)
             + jnp.einsum('bqhr,bkr->bhqk', qr_b, k_rope)).astype(jnp.float32) * scale
        q_pos = b_idx * Q_BLOCK + jnp.arange(Q_BLOCK)
        mask = key_pos[None, :] <= q_pos[:, None]
        s = jnp.where(mask, s, -jnp.inf)
        p = jax.nn.softmax(s, axis=-1).astype(v.dtype)
        return jnp.einsum('bhqk,bkhd->bqhd', p, v)

    o = lax.map(block, (qn, qr, jnp.arange(nb)))
    o = o.transpose(1, 0, 2, 3, 4).reshape(B, S, MLA_HEADS * MLA_V)
    return o @ w_o


def rglru_mixer(h, w_in, conv_w, conv_b, w_a, b_a, w_x, b_x, lam, w_out):
    B, S, _ = h.shape
    gate_br, x_br = jnp.split(h @ w_in, 2, axis=-1)
    gate_br = jax.nn.gelu(gate_br)
    xc = lax.conv_general_dilated(
        x_br, conv_w.reshape(CONV_WIDTH, 1, LRU_WIDTH).astype(x_br.dtype),
        window_strides=(1,), padding=[(CONV_WIDTH - 1, 0)],
        dimension_numbers=('NWC', 'WIO', 'NWC'), feature_group_count=LRU_WIDTH) + conv_b
    xh = xc.reshape(B, S, LRU_HEADS, LRU_HEAD_DIM)
    r = jax.nn.sigmoid((jnp.einsum('bshi,hij->bshj', xh, w_a).reshape(B, S, LRU_WIDTH) + b_a).astype(jnp.float32))
    i_g = jax.nn.sigmoid((jnp.einsum('bshi,hij->bshj', xh, w_x).reshape(B, S, LRU_WIDTH) + b_x).astype(jnp.float32))
    log_a = -LRU_C * r * jax.nn.softplus(-lam.astype(jnp.float32))
    a = jnp.exp(log_a)
    mult = jnp.sqrt(-jnp.expm1(2.0 * log_a))
    b = mult * (i_g * xc.astype(jnp.float32))

    def combine(left, right):
        a_l, b_l = left
        a_r, b_r = right
        return a_l * a_r, a_r * b_l + b_r

    _, hs = lax.associative_scan(combine, (a, b), axis=1)
    y = hs.astype(h.dtype) * gate_br
    return y @ w_out


def pool_mixer(h, w_grp, scale):
    B, S, D = h.shape
    hf = h.astype(jnp.float32)
    cs = jnp.pad(jnp.cumsum(hf, axis=1), ((0, 0), (1, 0), (0, 0)))
    t = jnp.arange(S)
    groups = []
    for g, w in enumerate(POOL_WINDOWS):
        c = cs[..., g * POOL_GROUP:(g + 1) * POOL_GROUP]
        upper = c[:, 1:]
        lower = jnp.pad(c[:, :S + 1 - w], ((0, 0), (w - 1, 0), (0, 0)))
        count = jnp.minimum(t + 1, w).astype(jnp.float32)[None, :, None]
        groups.append((upper - lower) / count - hf[..., g * POOL_GROUP:(g + 1) * POOL_GROUP])
    pooled = jnp.stack(groups, axis=2).astype(h.dtype)
    out = jnp.einsum('bsgc,gcd->bsgd', pooled, w_grp).reshape(B, S, D)
    return out * scale


def gmlp_mixer(h, w_in, ln_g, ln_b, w_s, b_s, w_out):
    B, S, _ = h.shape
    z = jax.nn.gelu(h @ w_in)
    u, v = jnp.split(z, 2, axis=-1)
    v = layernorm(v, ln_g, ln_b)
    nc = S // GMLP_CHUNK
    vg = v.reshape(B, nc, GMLP_CHUNK, GMLP_GROUPS, GMLP_GROUP_DIM)
    causal = jnp.tril(jnp.ones((GMLP_CHUNK, GMLP_CHUNK), dtype=bool))
    ws = jnp.where(causal[None], w_s, jnp.zeros_like(w_s))
    v2 = jnp.einsum('gts,bnsgc->bntgc', ws, vg) + b_s.T[:, :, None]
    v2 = v2.reshape(B, S, GMLP_WIDTH)
    return (u * v2) @ w_out


def swiglu(h, w_gate, w_up, w_down):
    return (jax.nn.silu(h @ w_gate) * (h @ w_up)) @ w_down


def moe_swiglu(h, w_router, w_gate, w_up, w_down):
    B, S, D = h.shape
    T = B * S
    A = T * TOP_K
    xt = h.reshape(T, D)
    logits = (xt @ w_router).astype(jnp.float32)
    top_val, top_idx = lax.top_k(logits, TOP_K)
    gates = jax.nn.softmax(top_val, axis=-1)
    exp_flat = top_idx.reshape(A)
    tok_flat = jnp.arange(A, dtype=jnp.int32) // TOP_K
    gate_flat = gates.reshape(A)
    order = jnp.argsort(exp_flat)
    exp_sorted = exp_flat[order]
    tok_sorted = tok_flat[order]
    gate_sorted = gate_flat[order]
    sizes = jnp.bincount(exp_flat, length=N_EXPERTS)
    starts = jnp.cumsum(sizes) - sizes
    padded = (sizes + MOE_BLOCK - 1) // MOE_BLOCK * MOE_BLOCK
    pad_end = jnp.cumsum(padded)
    pad_start = pad_end - padded
    dest = pad_start[exp_sorted] + (jnp.arange(A) - starts[exp_sorted])
    n_blocks = -(-(A + N_EXPERTS * (MOE_BLOCK - 1)) // MOE_BLOCK)
    P = n_blocks * MOE_BLOCK
    buf = jnp.zeros((P, D), h.dtype).at[dest].set(xt[tok_sorted])
    block_start = jnp.arange(n_blocks) * MOE_BLOCK
    block_expert = jnp.minimum(jnp.sum(pad_end[None, :] <= block_start[:, None], axis=1), N_EXPERTS - 1)

    def expert_block(args):
        xb, e = args
        return swiglu(xb, w_gate[e], w_up[e], w_down[e])

    out = lax.map(expert_block, (buf.reshape(n_blocks, MOE_BLOCK, D), block_expert)).reshape(P, D)
    contrib = out[dest] * gate_sorted[:, None].astype(out.dtype)
    y = jnp.zeros((T, D), h.dtype).at[tok_sorted].add(contrib)
    return y.reshape(B, S, D)


def setup_inputs(seed: int = 0) -> dict:
    key = jax.random.key(seed)
    ks = iter(jax.random.split(key, 40))

    def nrm(shape, fan_in):
        return jax.random.normal(next(ks), shape, jnp.float32) * fan_in ** -0.5

    def gain(shape):
        return 1.0 + 0.02 * jax.random.normal(next(ks), shape, jnp.float32)

    def bias(shape):
        return 0.02 * jax.random.normal(next(ks), shape, jnp.float32)

    n_mla, n_lru, n_pool, n_gmlp = (_count(m, N_MIXERS) for m in range(N_MIXERS))
    n_dense, n_moe = _count(0, 2), _count(1, 2)
    D = D_MODEL
    x = jax.random.normal(next(ks), (BATCH, SEQ, D), jnp.float32)
    positions = jnp.broadcast_to(jnp.arange(SEQ, dtype=jnp.int32)[None, :], (BATCH, SEQ))
    inp = {
        "x": x,
        "positions": positions,
        "norm_mix": gain((DEPTH, D)),
        "norm_ffn": gain((DEPTH, D)),
        "norm_final": gain((D,)),
        "mla_w_in": nrm((n_mla, D, MLA_Q_RANK + MLA_KV_RANK + MLA_ROPE), D),
        "mla_q_norm": gain((n_mla, MLA_Q_RANK)),
        "mla_kv_norm": gain((n_mla, MLA_KV_RANK)),
        "mla_w_uq": nrm((n_mla, MLA_Q_RANK, MLA_HEADS * (MLA_NOPE + MLA_ROPE)), MLA_Q_RANK),
        "mla_w_ukv": nrm((n_mla, MLA_KV_RANK, MLA_HEADS * (MLA_NOPE + MLA_V)), MLA_KV_RANK),
        "mla_w_o": nrm((n_mla, MLA_HEADS * MLA_V, D), MLA_HEADS * MLA_V),
        "lru_w_in": nrm((n_lru, D, 2 * LRU_WIDTH), D),
        "lru_conv_w": nrm((n_lru, CONV_WIDTH, LRU_WIDTH), CONV_WIDTH),
        "lru_conv_b": bias((n_lru, LRU_WIDTH)),
        "lru_w_a": nrm((n_lru, LRU_HEADS, LRU_HEAD_DIM, LRU_HEAD_DIM), LRU_HEAD_DIM),
        "lru_b_a": bias((n_lru, LRU_WIDTH)),
        "lru_w_x": nrm((n_lru, LRU_HEADS, LRU_HEAD_DIM, LRU_HEAD_DIM), LRU_HEAD_DIM),
        "lru_b_x": bias((n_lru, LRU_WIDTH)),
    }
    u = jax.random.uniform(next(ks), (n_lru, LRU_WIDTH), jnp.float32, 0.9, 0.999)
    a0 = u ** (1.0 / LRU_C)
    inp["lru_lam"] = jnp.log(a0) - jnp.log1p(-a0)
    inp.update({
        "lru_w_out": nrm((n_lru, LRU_WIDTH, D), LRU_WIDTH),
        "pool_w_grp": nrm((n_pool, len(POOL_WINDOWS), POOL_GROUP, POOL_GROUP), POOL_GROUP),
        "pool_scale": gain((n_pool, D)),
        "gmlp_w_in": nrm((n_gmlp, D, 2 * GMLP_WIDTH), D),
        "gmlp_ln_g": gain((n_gmlp, GMLP_WIDTH)),
        "gmlp_ln_b": bias((n_gmlp, GMLP_WIDTH)),
        "gmlp_w_s": nrm((n_gmlp, GMLP_GROUPS, GMLP_CHUNK, GMLP_CHUNK), GMLP_CHUNK),
        "gmlp_b_s": gain((n_gmlp, GMLP_GROUPS, GMLP_CHUNK)),
        "gmlp_w_out": nrm((n_gmlp, GMLP_WIDTH, D), GMLP_WIDTH),
        "ffn_w_gate": nrm((n_dense, D, D_FF), D),
        "ffn_w_up": nrm((n_dense, D, D_FF), D),
        "ffn_w_down": nrm((n_dense, D_FF, D), D_FF),
        "moe_w_router": nrm((n_moe, D, N_EXPERTS), D),
        "moe_w_gate": nrm((n_moe, N_EXPERTS, D, D_FF), D),
        "moe_w_up": nrm((n_moe, N_EXPERTS, D, D_FF), D),
        "moe_w_down": nrm((n_moe, N_EXPERTS, D_FF, D), D_FF),
    })
    return inp


def reference(x, positions, norm_mix, norm_ffn, norm_final,
              mla_w_in, mla_q_norm, mla_kv_norm, mla_w_uq, mla_w_ukv, mla_w_o,
              lru_w_in, lru_conv_w, lru_conv_b, lru_w_a, lru_b_a, lru_w_x, lru_b_x, lru_lam, lru_w_out,
              pool_w_grp, pool_scale,
              gmlp_w_in, gmlp_ln_g, gmlp_ln_b, gmlp_w_s, gmlp_b_s, gmlp_w_out,
              ffn_w_gate, ffn_w_up, ffn_w_down,
              moe_w_router, moe_w_gate, moe_w_up, moe_w_down):
    h = x
    for i in range(DEPTH):
        m, j = i % N_MIXERS, i // N_MIXERS
        hn = rmsnorm(h, norm_mix[i])
        if m == 0:
            mix = mla_mixer(hn, positions, mla_w_in[j], mla_q_norm[j], mla_kv_norm[j],
                            mla_w_uq[j], mla_w_ukv[j], mla_w_o[j])
        elif m == 1:
            mix = rglru_mixer(hn, lru_w_in[j], lru_conv_w[j], lru_conv_b[j], lru_w_a[j], lru_b_a[j],
                              lru_w_x[j], lru_b_x[j], lru_lam[j], lru_w_out[j])
        elif m == 2:
            mix = pool_mixer(hn, pool_w_grp[j], pool_scale[j])
        else:
            mix = gmlp_mixer(hn, gmlp_w_in[j], gmlp_ln_g[j], gmlp_ln_b[j], gmlp_w_s[j],
                             gmlp_b_s[j], gmlp_w_out[j])
        h = h + mix
        hn = rmsnorm(h, norm_ffn[i])
        k = i // 2
        if i % 2 == 0:
            h = h + swiglu(hn, ffn_w_gate[k], ffn_w_up[k], ffn_w_down[k])
        else:
            h = h + moe_swiglu(hn, moe_w_router[k], moe_w_gate[k], moe_w_up[k], moe_w_down[k])
    return rmsnorm(h, norm_final)
```

```python
import functools

import jax
import jax.numpy as jnp
from jax import lax
from jax.experimental import pallas as pl
from jax.experimental.pallas import tpu as pltpu

F32 = jnp.float32
BF16 = jnp.bfloat16
I32 = jnp.int32

RMS_EPS = 1e-6
LN_EPS = 1e-5

MLA_HEADS = 16
MLA_Q_RANK = 384
MLA_KV_RANK = 256
MLA_NOPE = 64
MLA_ROPE = 32
MLA_V = 64
ROPE_BASE = 10000.0
LRU_HEADS = 8
CONV_WIDTH = 4
LRU_C = 8.0
POOL_WINDOWS = (2, 4, 8, 16)
GMLP_GROUPS = 4
GMLP_CHUNK = 128
N_EXPERTS = 8
TOP_K = 2

LANES = 128
SUBLANES = 8
VMEM_LIMIT_BYTES = 56 * 1024 * 1024

TM_PROJ = 512
TQ_ATTN = 512
TM_RES = 512
TM_FFN = 1024
TF_FFN = 512
TT_LRU = 512
TM_POOL = 512
TM_GMLP = 512
TM_ROUTE = 512
TM_MOE = 1024
TM_DISPATCH = 512
TM_COMBINE = 256
POOL_HALO = 16


def _params(*sem):
    return pltpu.CompilerParams(dimension_semantics=sem, vmem_limit_bytes=VMEM_LIMIT_BYTES)


def _rmsnorm(x, g):
    return x * lax.rsqrt(jnp.mean(x * x, axis=-1, keepdims=True) + RMS_EPS) * g


def _dot(a, b):
    return jnp.dot(a, b, preferred_element_type=F32)


def _const_spec(shape):
    return pl.BlockSpec(shape, lambda *_: (0,) * len(shape))


def _mla_proj_kernel(h_ref, pos_ref, nw_ref, win_ref, qn_ref, kvn_ref, wuq_ref, wuk_ref, wuv_ref,
                     invf_ref, q_ref, k_ref, v_ref, *, scale):
    x = h_ref[...]
    xn = _rmsnorm(x, nw_ref[...]).astype(BF16)
    lat = _dot(xn, win_ref[...])
    cq = _rmsnorm(lat[:, :MLA_Q_RANK], qn_ref[...]).astype(BF16)
    ckv = _rmsnorm(lat[:, MLA_Q_RANK:MLA_Q_RANK + MLA_KV_RANK], kvn_ref[...]).astype(BF16)
    kr = lat[:, MLA_Q_RANK + MLA_KV_RANK:]
    ang = pos_ref[...].astype(F32) * invf_ref[...]
    c = jnp.cos(ang)
    s = jnp.sin(ang)
    lane = lax.broadcasted_iota(I32, ang.shape, 1)
    half = MLA_ROPE // 2
    s_lo = jnp.where(lane < MLA_NOPE + half, -s, 0.0)
    s_hi = jnp.where(lane >= MLA_NOPE + half, s, 0.0)

    def rope(t):
        return t * c + pltpu.roll(t, LANES - half, 1) * s_lo + pltpu.roll(t, half, 1) * s_hi

    kr = rope(kr)
    q = _dot(cq, wuq_ref[...])
    kn = _dot(ckv, wuk_ref[...])
    v = _dot(ckv, wuv_ref[...])
    for hh in range(MLA_HEADS):
        sl = slice(hh * LANES, (hh + 1) * LANES)
        q_ref[0, hh] = (rope(q[:, sl]) * scale).astype(BF16)
        k_ref[0, hh] = (kn[:, sl] + kr).astype(BF16)
    for hp in range(MLA_HEADS // 2):
        v_ref[0, hp] = v[:, hp * LANES:(hp + 1) * LANES].astype(BF16)


def _mla_proj(h, pos, nw, w_in, q_norm, kv_norm, w_uq, w_ukv, B, S):
    T, D = h.shape
    H = MLA_HEADS
    tm = min(TM_PROJ, S)
    ns = S // tm
    qk = MLA_NOPE + MLA_ROPE
    pad_r = LANES - qk
    w_in_p = jnp.concatenate([
        w_in[:, :MLA_Q_RANK + MLA_KV_RANK],
        jnp.zeros((D, MLA_NOPE), F32), w_in[:, MLA_Q_RANK + MLA_KV_RANK:], jnp.zeros((D, pad_r), F32)],
        axis=1).astype(BF16)
    wq = w_uq.reshape(MLA_Q_RANK, H, qk)
    wq_p = jnp.pad(wq, ((0, 0), (0, 0), (0, pad_r))).reshape(MLA_Q_RANK, H * LANES).astype(BF16)
    wkv = w_ukv.reshape(MLA_KV_RANK, H, MLA_NOPE + MLA_V)
    wk_p = jnp.pad(wkv[:, :, :MLA_NOPE], ((0, 0), (0, 0), (0, LANES - MLA_NOPE)))
    wk_p = wk_p.reshape(MLA_KV_RANK, H * LANES).astype(BF16)
    wv = wkv[:, :, MLA_NOPE:].reshape(MLA_KV_RANK, H * MLA_V).astype(BF16)
    half = MLA_ROPE // 2
    inv_freq = ROPE_BASE ** (-jnp.arange(half, dtype=F32) / half)
    invf = jnp.concatenate([jnp.zeros((MLA_NOPE,), F32), inv_freq, inv_freq, jnp.zeros((pad_r,), F32)])
    invf = invf.reshape(1, LANES)
    wl = w_in_p.shape[1]
    out_shape = (
        jax.ShapeDtypeStruct((B, H, S, LANES), BF16),
        jax.ShapeDtypeStruct((B, H, S, LANES), BF16),
        jax.ShapeDtypeStruct((B, H // 2, S, LANES), BF16),
    )
    head_spec = pl.BlockSpec((1, H, tm, LANES), lambda i: (i // ns, 0, i % ns, 0))
    return pl.pallas_call(
        functools.partial(_mla_proj_kernel, scale=qk ** -0.5),
        out_shape=out_shape,
        grid=(T // tm,),
        in_specs=[
            pl.BlockSpec((tm, D), lambda i: (i, 0)),
            pl.BlockSpec((tm, 1), lambda i: (i, 0)),
            _const_spec((1, D)),
            _const_spec((D, wl)),
            _const_spec((1, MLA_Q_RANK)),
            _const_spec((1, MLA_KV_RANK)),
            _const_spec((MLA_Q_RANK, H * LANES)),
            _const_spec((MLA_KV_RANK, H * LANES)),
            _const_spec((MLA_KV_RANK, H * MLA_V)),
            _const_spec((1, LANES)),
        ],
        out_specs=(head_spec, head_spec,
                   pl.BlockSpec((1, H // 2, tm, LANES), lambda i: (i // ns, 0, i % ns, 0))),
        compiler_params=_params("parallel"),
        name="mla_proj",
    )(h, pos, nw.reshape(1, D), w_in_p, q_norm.reshape(1, -1), kv_norm.reshape(1, -1), wq_p, wk_p, wv,
      invf)


def _attn_kernel(q_ref, k_ref, v_ref, o_ref, m_scr, l_scr, acc_scr, *, tq):
    qi = pl.program_id(2)
    outs = []
    for hh in range(2):
        q = q_ref[0, hh]
        m_scr[...] = jnp.full(m_scr.shape, -jnp.inf, F32)
        l_scr[...] = jnp.zeros(l_scr.shape, F32)
        acc_scr[...] = jnp.zeros(acc_scr.shape, F32)

        def step(j, masked, q=q, hh=hh):
            start = pl.multiple_of(j * tq, tq)
            k = k_ref[0, hh, pl.ds(start, tq), :]
            s = lax.dot_general(q, k, (((1,), (1,)), ((), ())), preferred_element_type=F32)
            if masked:
                row = lax.broadcasted_iota(I32, s.shape, 0)
                col = lax.broadcasted_iota(I32, s.shape, 1)
                s = jnp.where(col <= row, s, -jnp.inf)
            m_prev = m_scr[...]
            m_new = jnp.maximum(m_prev, jnp.max(s, axis=1, keepdims=True))
            alpha = jnp.exp(m_prev - m_new)
            p = jnp.exp(s - m_new)
            l_scr[...] = alpha * l_scr[...] + jnp.sum(p, axis=1, keepdims=True)
            pv = _dot(p.astype(BF16), v_ref[0, 0, pl.ds(start, tq), :])
            acc_scr[...] = alpha * acc_scr[...] + pv
            m_scr[...] = m_new

        def body(j, carry):
            step(j, False)
            return carry

        lax.fori_loop(0, qi, body, 0)
        step(qi, True)
        outs.append(acc_scr[...] / l_scr[...])
    lane = lax.broadcasted_iota(I32, outs[0].shape, 1)
    o_ref[0] = jnp.where(lane < MLA_V, outs[0], outs[1]).astype(BF16)


def _attention(q, k, v):
    B, H, S, _ = q.shape
    tq = min(TQ_ATTN, S)
    return pl.pallas_call(
        functools.partial(_attn_kernel, tq=tq),
        out_shape=jax.ShapeDtypeStruct((B, S, H * MLA_V), BF16),
        grid=(B, H // 2, S // tq),
        in_specs=[
            pl.BlockSpec((1, 2, tq, LANES), lambda b, hp, qi: (b, hp, qi, 0)),
            pl.BlockSpec((1, 2, S, LANES), lambda b, hp, qi: (b, hp, 0, 0)),
            pl.BlockSpec((1, 1, S, LANES), lambda b, hp, qi: (b, hp, 0, 0)),
        ],
        out_specs=pl.BlockSpec((1, tq, LANES), lambda b, hp, qi: (b, qi, hp)),
        scratch_shapes=[pltpu.VMEM((tq, 1), F32), pltpu.VMEM((tq, 1), F32), pltpu.VMEM((tq, LANES), F32)],
        compiler_params=_params("parallel", "parallel", "arbitrary"),
        name="mla_attention",
    )(q, k, v)


def _matmul_res_kernel(x_ref, w_ref, h_ref, o_ref):
    o_ref[...] = h_ref[...] + _dot(x_ref[...], w_ref[...])


def _matmul_residual(x, w, h):
    T, K = x.shape
    N = w.shape[1]
    tm = min(TM_RES, T)
    return pl.pallas_call(
        _matmul_res_kernel,
        out_shape=jax.ShapeDtypeStruct((T, N), F32),
        grid=(T // tm,),
        in_specs=[pl.BlockSpec((tm, K), lambda i: (i, 0)), _const_spec((K, N)),
                  pl.BlockSpec((tm, N), lambda i: (i, 0))],
        out_specs=pl.BlockSpec((tm, N), lambda i: (i, 0)),
        compiler_params=_params("parallel"),
        name="matmul_residual",
    )(x, w.astype(BF16), h)


def _swiglu_steps(x_ref, nw_ref, wg_ref, wu_ref, wd_ref, xn_scr, acc_scr):
    j = pl.program_id(1)

    @pl.when(j == 0)
    def _():
        xn_scr[...] = _rmsnorm(x_ref[...], nw_ref[...]).astype(BF16)
        acc_scr[...] = jnp.zeros(acc_scr.shape, F32)

    xn = xn_scr[...]
    g = _dot(xn, wg_ref[...])
    u = _dot(xn, wu_ref[...])
    hmid = (g * jax.nn.sigmoid(g) * u).astype(BF16)
    acc_scr[...] += _dot(hmid, wd_ref[...])


def _ffn_dense_kernel(x_ref, nw_ref, wg_ref, wu_ref, wd_ref, o_ref, xn_scr, acc_scr):
    _swiglu_steps(x_ref, nw_ref, wg_ref, wu_ref, wd_ref, xn_scr, acc_scr)

    @pl.when(pl.program_id(1) == pl.num_programs(1) - 1)
    def _():
        o_ref[...] = x_ref[...] + acc_scr[...]


def _ffn_dense(h, nw, wg, wu, wd):
    T, D = h.shape
    FF = wg.shape[1]
    tm = min(TM_FFN, T)
    tf = min(TF_FFN, FF)
    return pl.pallas_call(
        _ffn_dense_kernel,
        out_shape=jax.ShapeDtypeStruct((T, D), F32),
        grid=(T // tm, FF // tf),
        in_specs=[
            pl.BlockSpec((tm, D), lambda i, j: (i, 0)),
            _const_spec((1, D)),
            pl.BlockSpec((D, tf), lambda i, j: (0, j)),
            pl.BlockSpec((D, tf), lambda i, j: (0, j)),
            pl.BlockSpec((tf, D), lambda i, j: (j, 0)),
        ],
        out_specs=pl.BlockSpec((tm, D), lambda i, j: (i, 0)),
        scratch_shapes=[pltpu.VMEM((tm, D), BF16), pltpu.VMEM((tm, D), F32)],
        compiler_params=_params("parallel", "arbitrary"),
        name="ffn_dense",
    )(h, nw.reshape(1, D), wg.astype(BF16), wu.astype(BF16), wd.astype(BF16))


def _ffn_moe_kernel(te_ref, nv_ref, x_ref, nw_ref, wg_ref, wu_ref, wd_ref, o_ref, xn_scr, acc_scr):
    del te_ref
    valid = pl.program_id(0) < nv_ref[0]
    last = pl.program_id(1) == pl.num_programs(1) - 1

    @pl.when(valid)
    def _():
        _swiglu_steps(x_ref, nw_ref, wg_ref, wu_ref, wd_ref, xn_scr, acc_scr)

        @pl.when(last)
        def _():
            o_ref[...] = acc_scr[...]

    @pl.when(jnp.logical_and(jnp.logical_not(valid), last))
    def _():
        o_ref[...] = jnp.zeros(o_ref.shape, F32)


def _ffn_moe(xs, nw, wg, wu, wd, tile_expert, n_valid, tm):
    P, D = xs.shape
    FF = wg.shape[2]
    tf = min(TF_FFN, FF)
    nj = FF // tf

    def row_map(i, j, te, nv):
        return (jnp.minimum(i, nv[0] - 1), 0)

    def col(i, j, nv):
        return jnp.where(i < nv[0], j, nj - 1)

    def up_map(i, j, te, nv):
        return (te[jnp.minimum(i, nv[0] - 1)], 0, col(i, j, nv))

    def down_map(i, j, te, nv):
        return (te[jnp.minimum(i, nv[0] - 1)], col(i, j, nv), 0)

    grid_spec = pltpu.PrefetchScalarGridSpec(
        num_scalar_prefetch=2,
        grid=(P // tm, nj),
        in_specs=[
            pl.BlockSpec((tm, D), row_map),
            pl.BlockSpec((1, D), lambda i, j, te, nv: (0, 0)),
            pl.BlockSpec((None, D, tf), up_map),
            pl.BlockSpec((None, D, tf), up_map),
            pl.BlockSpec((None, tf, D), down_map),
        ],
        out_specs=pl.BlockSpec((tm, D), lambda i, j, te, nv: (i, 0)),
        scratch_shapes=[pltpu.VMEM((tm, D), BF16), pltpu.VMEM((tm, D), F32)],
    )
    return pl.pallas_call(
        _ffn_moe_kernel,
        out_shape=jax.ShapeDtypeStruct((P, D), F32),
        grid_spec=grid_spec,
        compiler_params=_params("arbitrary", "arbitrary"),
        name="ffn_moe",
    )(tile_expert, n_valid, xs, nw.reshape(1, D), wg.astype(BF16), wu.astype(BF16), wd.astype(BF16))


def _lru_kernel(h_ref, nw_ref, win_ref, cw_ref, cb_ref, wax_ref, ba_ref, bx_ref, lam_ref, wout_ref,
                o_ref, halo_scr, carry_scr, a_scr, b_scr, hs_scr):
    tt, W = h_ref.shape[1], a_scr.shape[2]
    G = tt // SUBLANES

    @pl.when(pl.program_id(1) == 0)
    def _():
        halo_scr[...] = jnp.zeros(halo_scr.shape, F32)
        carry_scr[...] = jnp.zeros(carry_scr.shape, F32)

    x = h_ref[0]
    xn = _rmsnorm(x, nw_ref[...]).astype(BF16)
    z = _dot(xn, win_ref[...])
    gate = jax.nn.gelu(z[:, :W])
    xb = z[:, W:]
    xe = jnp.concatenate([halo_scr[...], xb], axis=0)
    halo_scr[...] = xb[tt - SUBLANES:, :]
    cw = cw_ref[...]
    xc = cb_ref[...]
    for kk in range(CONV_WIDTH):
        off = SUBLANES - (CONV_WIDTH - 1) + kk
        xc = xc + cw[kk:kk + 1, :] * xe[off:off + tt, :]
    xcb = xc.astype(BF16)
    hd = W // LRU_HEADS
    r_parts, i_parts = [], []
    for n in range(LRU_HEADS):
        ri = _dot(xcb[:, n * hd:(n + 1) * hd], wax_ref[n])
        r_parts.append(ri[:, :hd])
        i_parts.append(ri[:, hd:])
    r = jax.nn.sigmoid(jnp.concatenate(r_parts, axis=1) + ba_ref[...])
    ig = jax.nn.sigmoid(jnp.concatenate(i_parts, axis=1) + bx_ref[...])
    nl = -lam_ref[...]
    softplus = jnp.maximum(nl, 0.0) + jnp.log1p(jnp.exp(-jnp.abs(nl)))
    a = jnp.exp(-LRU_C * r * softplus)
    b = jnp.sqrt(1.0 - a * a) * (ig * xc)

    A = a.reshape(G, SUBLANES, W)
    Bv = b.reshape(G, SUBLANES, W)
    sub = lax.broadcasted_iota(I32, A.shape, 1)
    for d in (1, 2, 4):
        keep = sub >= d
        a_sh = jnp.where(keep, pltpu.roll(A, d, 1), 1.0)
        b_sh = jnp.where(keep, pltpu.roll(Bv, d, 1), 0.0)
        Bv = A * b_sh + Bv
        A = A * a_sh
    a_scr[...] = A
    b_scr[...] = Bv

    def body(g, hprev):
        hg = b_scr[g] + a_scr[g] * hprev
        hs_scr[g] = hg
        return hg[SUBLANES - 1:, :]

    carry_scr[...] = lax.fori_loop(0, G, body, carry_scr[...], unroll=8)
    y = (hs_scr[...].reshape(tt, W) * gate).astype(BF16)
    o_ref[0] = x + _dot(y, wout_ref[...])


def _lru_mixer(h, nw, w_in, conv_w, conv_b, w_a, b_a, w_x, b_x, lam, w_out):
    B, S, D = h.shape
    W = w_out.shape[0]
    tt = min(TT_LRU, S)
    hd = W // LRU_HEADS
    wax = jnp.concatenate([w_a, w_x], axis=2).astype(BF16)
    G = tt // SUBLANES
    return pl.pallas_call(
        _lru_kernel,
        out_shape=jax.ShapeDtypeStruct((B, S, D), F32),
        grid=(B, S // tt),
        in_specs=[
            pl.BlockSpec((1, tt, D), lambda b, t: (b, t, 0)),
            _const_spec((1, D)),
            _const_spec((D, 2 * W)),
            _const_spec((CONV_WIDTH, W)),
            _const_spec((1, W)),
            _const_spec((LRU_HEADS, hd, 2 * hd)),
            _const_spec((1, W)),
            _const_spec((1, W)),
            _const_spec((1, W)),
            _const_spec((W, D)),
        ],
        out_specs=pl.BlockSpec((1, tt, D), lambda b, t: (b, t, 0)),
        scratch_shapes=[
            pltpu.VMEM((SUBLANES, W), F32), pltpu.VMEM((1, W), F32),
            pltpu.VMEM((G, SUBLANES, W), F32), pltpu.VMEM((G, SUBLANES, W), F32),
            pltpu.VMEM((G, SUBLANES, W), F32),
        ],
        compiler_params=_params("parallel", "arbitrary"),
        name="rglru_mixer",
    )(h, nw.reshape(1, D), w_in.astype(BF16), conv_w, conv_b.reshape(1, W), wax, b_a.reshape(1, W),
      b_x.reshape(1, W), lam.reshape(1, W), w_out.astype(BF16))


def _pool_kernel(h_ref, halo_ref, nw_ref, wg_ref, sc_ref, o_ref, *, blocks_per_seq):
    tm, D = h_ref.shape
    gsz = D // len(POOL_WINDOWS)
    x = h_ref[...]
    first = pl.program_id(0) % blocks_per_seq == 0
    xn = _rmsnorm(x, nw_ref[...])
    hn_halo = _rmsnorm(halo_ref[...], nw_ref[...])
    hn_halo = jnp.where(first, 0.0, hn_halo)
    e = jnp.concatenate([hn_halo, xn], axis=0)
    t = (pl.program_id(0) % blocks_per_seq) * tm + lax.broadcasted_iota(I32, (tm, 1), 0)
    outs = []
    for g, w in enumerate(POOL_WINDOWS):
        cur = e[:, g * gsz:(g + 1) * gsz]
        base = 0
        d = 1
        while d < w:
            cur = cur[d:, :] + cur[:-d, :]
            base += d
            d *= 2
        win = cur[POOL_HALO - base:POOL_HALO - base + tm, :]
        count = jnp.minimum(t + 1, w).astype(F32)
        pooled = (win / count - xn[:, g * gsz:(g + 1) * gsz]).astype(BF16)
        outs.append(_dot(pooled, wg_ref[g]))
    o_ref[...] = x + jnp.concatenate(outs, axis=1) * sc_ref[...]


def _pool_mixer(h, nw, w_grp, scale, S):
    T, D = h.shape
    tm = min(TM_POOL, S)
    bps = S // tm
    hb = tm // POOL_HALO
    ng, gsz = w_grp.shape[0], w_grp.shape[1]
    return pl.pallas_call(
        functools.partial(_pool_kernel, blocks_per_seq=bps),
        out_shape=jax.ShapeDtypeStruct((T, D), F32),
        grid=(T // tm,),
        in_specs=[
            pl.BlockSpec((tm, D), lambda i: (i, 0)),
            pl.BlockSpec((POOL_HALO, D), lambda i: (jnp.maximum(i * hb - 1, 0), 0)),
            _const_spec((1, D)),
            _const_spec((ng, gsz, gsz)),
            _const_spec((1, D)),
        ],
        out_specs=pl.BlockSpec((tm, D), lambda i: (i, 0)),
        compiler_params=_params("parallel"),
        name="pool_mixer",
    )(h, h, nw.reshape(1, D), w_grp.astype(BF16), scale.reshape(1, D))


def _gmlp_kernel(h_ref, nw_ref, win_ref, lg_ref, lb_ref, ws_ref, bs_ref, wout_ref, o_ref):
    tm, D = h_ref.shape
    W = wout_ref.shape[0]
    gw = W // GMLP_GROUPS
    x = h_ref[...]
    xn = _rmsnorm(x, nw_ref[...]).astype(BF16)
    z = jax.nn.gelu(_dot(xn, win_ref[...]))
    u = z[:, :W]
    v = z[:, W:]
    mu = jnp.mean(v, axis=-1, keepdims=True)
    vc = v - mu
    var = jnp.mean(vc * vc, axis=-1, keepdims=True)
    vb = (vc * lax.rsqrt(var + LN_EPS) * lg_ref[...] + lb_ref[...]).astype(BF16)
    row = lax.broadcasted_iota(I32, (GMLP_CHUNK, GMLP_CHUNK), 0)
    col = lax.broadcasted_iota(I32, (GMLP_CHUNK, GMLP_CHUNK), 1)
    bs = bs_ref[...]
    ws = [jnp.where(col <= row, ws_ref[g], 0.0).astype(BF16) for g in range(GMLP_GROUPS)]
    rows = []
    for c in range(tm // GMLP_CHUNK):
        cols = []
        for g in range(GMLP_GROUPS):
            blk = vb[c * GMLP_CHUNK:(c + 1) * GMLP_CHUNK, g * gw:(g + 1) * gw]
            cols.append(_dot(ws[g], blk) + bs[:, g:g + 1])
        rows.append(jnp.concatenate(cols, axis=1))
    v2 = jnp.concatenate(rows, axis=0)
    o_ref[...] = x + _dot((u * v2).astype(BF16), wout_ref[...])


def _gmlp_mixer(h, nw, w_in, ln_g, ln_b, w_s, b_s, w_out):
    T, D = h.shape
    W = w_out.shape[0]
    tm = min(TM_GMLP, T)
    return pl.pallas_call(
        _gmlp_kernel,
        out_shape=jax.ShapeDtypeStruct((T, D), F32),
        grid=(T // tm,),
        in_specs=[
            pl.BlockSpec((tm, D), lambda i: (i, 0)),
            _const_spec((1, D)),
            _const_spec((D, 2 * W)),
            _const_spec((1, W)),
            _const_spec((1, W)),
            _const_spec((GMLP_GROUPS, GMLP_CHUNK, GMLP_CHUNK)),
            _const_spec((GMLP_CHUNK, GMLP_GROUPS)),
            _const_spec((W, D)),
        ],
        out_specs=pl.BlockSpec((tm, D), lambda i: (i, 0)),
        compiler_params=_params("parallel"),
        name="gmlp_mixer",
    )(h, nw.reshape(1, D), w_in.astype(BF16), ln_g.reshape(1, W), ln_b.reshape(1, W), w_s, b_s.T,
      w_out.astype(BF16))


def _router_kernel(h_ref, nw_ref, wr_ref, meta_ref, gate_ref, cnt_ref, cnt_scr):
    tm = h_ref.shape[0]

    @pl.when(pl.program_id(0) == 0)
    def _():
        cnt_scr[...] = jnp.zeros(cnt_scr.shape, F32)

    xn = _rmsnorm(h_ref[...], nw_ref[...])
    logits = jnp.dot(xn, wr_ref[...], precision=lax.Precision.HIGHEST, preferred_element_type=F32)
    lane = lax.broadcasted_iota(I32, logits.shape, 1)
    logits = jnp.where(lane < N_EXPERTS, logits, -jnp.inf)
    m1 = jnp.max(logits, axis=1, keepdims=True)
    e1 = jnp.min(jnp.where(logits == m1, lane, LANES), axis=1, keepdims=True)
    rest = jnp.where(lane == e1, -jnp.inf, logits)
    m2 = jnp.max(rest, axis=1, keepdims=True)
    e2 = jnp.min(jnp.where(rest == m2, lane, LANES), axis=1, keepdims=True)
    ex = jnp.exp(m2 - m1)
    g1 = 1.0 / (1.0 + ex)
    g2 = ex / (1.0 + ex)
    hit1 = lane == e1
    hit2 = lane == e2
    onehot = jnp.where(hit1 | hit2, 1.0, 0.0)
    row = lax.broadcasted_iota(I32, (tm, tm), 0)
    col = lax.broadcasted_iota(I32, (tm, tm), 1)
    tri = jnp.where(col < row, 1.0, 0.0).astype(BF16)
    before = _dot(tri, onehot.astype(BF16)) + cnt_scr[...]
    r1 = jnp.sum(jnp.where(hit1, before, 0.0), axis=1, keepdims=True).astype(I32)
    r2 = jnp.sum(jnp.where(hit2, before, 0.0), axis=1, keepdims=True).astype(I32)
    cnt_scr[...] += jnp.sum(onehot, axis=0, keepdims=True)
    meta = jnp.where(lane == 0, e1, jnp.where(lane == 1, e2, jnp.where(lane == 2, r1, r2)))
    meta_ref[...] = meta
    gate_ref[...] = jnp.where(lane == 0, g1, g2)
    cnt_ref[...] = cnt_scr[...]


def _router(h, nw, w_router):
    T, D = h.shape
    tm = min(TM_ROUTE, T)
    wr = jnp.pad(w_router, ((0, 0), (0, LANES - N_EXPERTS)))
    return pl.pallas_call(
        _router_kernel,
        out_shape=(jax.ShapeDtypeStruct((T, LANES), I32), jax.ShapeDtypeStruct((T, LANES), F32),
                   jax.ShapeDtypeStruct((1, LANES), F32)),
        grid=(T // tm,),
        in_specs=[pl.BlockSpec((tm, D), lambda i: (i, 0)), _const_spec((1, D)), _const_spec((D, LANES))],
        out_specs=(pl.BlockSpec((tm, LANES), lambda i: (i, 0)), pl.BlockSpec((tm, LANES), lambda i: (i, 0)),
                   _const_spec((1, LANES))),
        scratch_shapes=[pltpu.VMEM((1, LANES), F32)],
        compiler_params=_params("arbitrary"),
        name="moe_router",
    )(h, nw.reshape(1, D), wr)


def _dispatch_kernel(d1_ref, d2_ref, h_ref, xs_in_ref, xs_ref, sem):
    del xs_in_ref
    tm = h_ref.shape[0]

    def copies(r):
        src = h_ref.at[pl.ds(r, 1), :]
        return (pltpu.make_async_copy(src, xs_ref.at[pl.ds(d1_ref[0, 0, r], 1), :], sem),
                pltpu.make_async_copy(src, xs_ref.at[pl.ds(d2_ref[0, 0, r], 1), :], sem))

    def start(r, c):
        for cp in copies(r):
            cp.start()
        return c

    def wait(r, c):
        for cp in copies(r):
            cp.wait()
        return c

    lax.fori_loop(0, tm, start, 0)
    lax.fori_loop(0, tm, wait, 0)


def _dispatch(h, d1, d2, P):
    T, D = h.shape
    tm = min(TM_DISPATCH, T)
    nb = T // tm
    smem = functools.partial(pl.BlockSpec, memory_space=pltpu.SMEM)
    return pl.pallas_call(
        _dispatch_kernel,
        out_shape=jax.ShapeDtypeStruct((P, D), F32),
        grid=(nb,),
        in_specs=[
            smem((1, 1, tm), lambda i: (i, 0, 0)),
            smem((1, 1, tm), lambda i: (i, 0, 0)),
            pl.BlockSpec((tm, D), lambda i: (i, 0)),
            pl.BlockSpec(memory_space=pl.ANY),
        ],
        out_specs=pl.BlockSpec(memory_space=pl.ANY),
        scratch_shapes=[pltpu.SemaphoreType.DMA],
        input_output_aliases={3: 0},
        compiler_params=_params("arbitrary"),
        name="moe_dispatch",
    )(d1.reshape(nb, 1, tm), d2.reshape(nb, 1, tm), h, jnp.zeros((P, D), F32))


def _combine_kernel(d1_ref, d2_ref, h_ref, gate_ref, ys_ref, *rest, final_norm):
    if final_norm:
        fw_ref, o_ref, buf1, buf2, sem = rest
    else:
        o_ref, buf1, buf2, sem = rest
    tm = h_ref.shape[0]

    def copies(r):
        return (pltpu.make_async_copy(ys_ref.at[pl.ds(d1_ref[0, 0, r], 1), :], buf1.at[pl.ds(r, 1), :], sem),
                pltpu.make_async_copy(ys_ref.at[pl.ds(d2_ref[0, 0, r], 1), :], buf2.at[pl.ds(r, 1), :], sem))

    def start(r, c):
        for cp in copies(r):
            cp.start()
        return c

    def wait(r, c):
        for cp in copies(r):
            cp.wait()
        return c

    lax.fori_loop(0, tm, start, 0)
    lax.fori_loop(0, tm, wait, 0)
    gates = gate_ref[...]
    out = h_ref[...] + gates[:, 0:1] * buf1[...] + gates[:, 1:2] * buf2[...]
    if final_norm:
        out = _rmsnorm(out, fw_ref[...])
    o_ref[...] = out


def _combine(h, gates, ys, d1, d2, final_w=None):
    T, D = h.shape
    tm = min(TM_COMBINE, T)
    nb = T // tm
    smem = functools.partial(pl.BlockSpec, memory_space=pltpu.SMEM)
    in_specs = [
        smem((1, 1, tm), lambda i: (i, 0, 0)),
        smem((1, 1, tm), lambda i: (i, 0, 0)),
        pl.BlockSpec((tm, D), lambda i: (i, 0)),
        pl.BlockSpec((tm, LANES), lambda i: (i, 0)),
        pl.BlockSpec(memory_space=pl.ANY),
    ]
    args = [d1.reshape(nb, 1, tm), d2.reshape(nb, 1, tm), h, gates, ys]
    if final_w is not None:
        in_specs.append(_const_spec((1, D)))
        args.append(final_w.reshape(1, D))
    return pl.pallas_call(
        functools.partial(_combine_kernel, final_norm=final_w is not None),
        out_shape=jax.ShapeDtypeStruct((T, D), F32),
        grid=(nb,),
        in_specs=in_specs,
        out_specs=pl.BlockSpec((tm, D), lambda i: (i, 0)),
        scratch_shapes=[pltpu.VMEM((tm, D), F32), pltpu.VMEM((tm, D), F32), pltpu.SemaphoreType.DMA],
        compiler_params=_params("arbitrary"),
        name="moe_combine",
    )(*args)


def _moe_layer(h, nw, w_router, wg, wu, wd, final_w=None):
    T, D = h.shape
    meta, gates, counts = _router(h, nw, w_router)
    tm = min(TM_MOE, T)
    sizes = counts[0, :N_EXPERTS].astype(I32)
    padded = (sizes + tm - 1) // tm * tm
    pad_end = jnp.cumsum(padded)
    pad_start = pad_end - padded
    d1 = pad_start[meta[:, 0]] + meta[:, 2]
    d2 = pad_start[meta[:, 1]] + meta[:, 3]
    n_tiles = (T * TOP_K + N_EXPERTS * (tm - 1)) // tm
    tile_start = jnp.arange(n_tiles, dtype=I32) * tm
    tile_expert = jnp.minimum(jnp.sum(pad_end[None, :] <= tile_start[:, None], axis=1), N_EXPERTS - 1)
    n_valid = (pad_end[-1] // tm).reshape(1)
    xs = _dispatch(h, d1, d2, n_tiles * tm)
    ys = _ffn_moe(xs, nw, wg, wu, wd, tile_expert.astype(I32), n_valid.astype(I32), tm)
    return _combine(h, gates, ys, d1, d2, final_w)


def kernel(x, positions, norm_mix, norm_ffn, norm_final, mla_w_in, mla_q_norm, mla_kv_norm, mla_w_uq,
           mla_w_ukv, mla_w_o, lru_w_in, lru_conv_w, lru_conv_b, lru_w_a, lru_b_a, lru_w_x, lru_b_x,
           lru_lam, lru_w_out, pool_w_grp, pool_scale, gmlp_w_in, gmlp_ln_g, gmlp_ln_b, gmlp_w_s,
           gmlp_b_s, gmlp_w_out, ffn_w_gate, ffn_w_up, ffn_w_down, moe_w_router, moe_w_gate, moe_w_up,
           moe_w_down):
    B, S, D = x.shape
    T = B * S
    depth = norm_mix.shape[0]
    h = x.reshape(T, D)
    for i in range(depth):
        m, j = i % 4, i // 4
        if m == 0:
            q, k, v = _mla_proj(h, positions.reshape(T, 1), norm_mix[i], mla_w_in[j], mla_q_norm[j],
                                mla_kv_norm[j], mla_w_uq[j], mla_w_ukv[j], B, S)
            o = _attention(q, k, v)
            h = _matmul_residual(o.reshape(T, -1), mla_w_o[j], h)
        elif m == 1:
            h = _lru_mixer(h.reshape(B, S, D), norm_mix[i], lru_w_in[j], lru_conv_w[j], lru_conv_b[j],
                           lru_w_a[j], lru_b_a[j], lru_w_x[j], lru_b_x[j], lru_lam[j],
                           lru_w_out[j]).reshape(T, D)
        elif m == 2:
            h = _pool_mixer(h, norm_mix[i], pool_w_grp[j], pool_scale[j], S)
        else:
            h = _gmlp_mixer(h, norm_mix[i], gmlp_w_in[j], gmlp_ln_g[j], gmlp_ln_b[j], gmlp_w_s[j],
                            gmlp_b_s[j], gmlp_w_out[j])
        kk = i // 2
        if i % 2 == 0:
            h = _ffn_dense(h, norm_ffn[i], ffn_w_gate[kk], ffn_w_up[kk], ffn_w_down[kk])
        else:
            last = i == depth - 1
            h = _moe_layer(h, norm_ffn[i], moe_w_router[kk], moe_w_gate[kk], moe_w_up[kk], moe_w_down[kk],
                           norm_final if last else None)
    if depth % 2 == 1:
        raise NotImplementedError("final RMSNorm is fused into the last MoE combine")
    return h.reshape(B, S, D)
```

```python
import functools

import jax
import jax.numpy as jnp
from jax import lax
from jax.experimental import pallas as pl
from jax.experimental.pallas import tpu as pltpu

F32 = jnp.float32
BF16 = jnp.bfloat16
I32 = jnp.int32

RMS_EPS = 1e-6
LN_EPS = 1e-5

MLA_HEADS = 16
MLA_Q_RANK = 384
MLA_KV_RANK = 256
MLA_NOPE = 64
MLA_ROPE = 32
MLA_V = 64
ROPE_BASE = 10000.0
LRU_HEADS = 8
CONV_WIDTH = 4
LRU_C = 8.0
POOL_WINDOWS = (2, 4, 8, 16)
GMLP_GROUPS = 4
GMLP_CHUNK = 128
N_EXPERTS = 8
TOP_K = 2

LANES = 128
SUBLANES = 8
VMEM_LIMIT_BYTES = 56 * 1024 * 1024

TM_PROJ = 512
TQ_ATTN = 512
TM_RES = 512
TM_FFN = 1024
TF_FFN = 512
TT_LRU = 512
TM_POOL = 512
TM_GMLP = 512
TM_ROUTE = 512
TM_MOE = 1024
TM_DISPATCH = 512
TM_COMBINE = 256
POOL_HALO = 16


def _params(*sem):
    return pltpu.CompilerParams(dimension_semantics=sem, vmem_limit_bytes=VMEM_LIMIT_BYTES)


def _rmsnorm(x, g):
    return x * lax.rsqrt(jnp.mean(x * x, axis=-1, keepdims=True) + RMS_EPS) * g


def _dot(a, b):
    return jnp.dot(a, b, preferred_element_type=F32)


def _const_spec(shape):
    return pl.BlockSpec(shape, lambda *_: (0,) * len(shape))


def _mla_proj_kernel(h_ref, pos_ref, nw_ref, win_ref, qn_ref, kvn_ref, wuqt_ref, wuk_ref, wuvt_ref,
                     invf_ref, q_ref, k_ref, v_ref, *, scale):
    half = MLA_ROPE // 2
    x = h_ref[...]
    tm = x.shape[0]
    xn = _rmsnorm(x, nw_ref[...]).astype(BF16)
    lat = _dot(xn, win_ref[...])
    cq = _rmsnorm(lat[:, :MLA_Q_RANK], qn_ref[...]).astype(BF16)
    ckv = _rmsnorm(lat[:, MLA_Q_RANK:MLA_Q_RANK + MLA_KV_RANK], kvn_ref[...]).astype(BF16)
    kr = lat[:, MLA_Q_RANK + MLA_KV_RANK:]
    ang = invf_ref[...] * pos_ref[...].astype(F32)
    c = jnp.cos(ang)
    s = jnp.sin(ang)
    pad_r = LANES - MLA_NOPE - MLA_ROPE
    one = jnp.ones((MLA_NOPE, tm), F32)
    c_t = jnp.concatenate([one, c, c, jnp.ones((pad_r, tm), F32)], axis=0).T
    s_lo = jnp.concatenate([0.0 * one, -s, jnp.zeros((half + pad_r, tm), F32)], axis=0).T
    s_hi = jnp.concatenate([0.0 * one, 0.0 * s, s, jnp.zeros((pad_r, tm), F32)], axis=0).T
    kr = kr * c_t + pltpu.roll(kr, LANES - half, 1) * s_lo + pltpu.roll(kr, half, 1) * s_hi
    kn = _dot(ckv, wuk_ref[...])
    for hh in range(MLA_HEADS):
        k_ref[0, hh] = (kn[:, hh * LANES:(hh + 1) * LANES] + kr).astype(BF16)
    nt = (((1,), (1,)), ((), ()))
    qt = lax.dot_general(wuqt_ref[...], cq, nt, preferred_element_type=F32)
    for hh in range(MLA_HEADS):
        blk = qt[hh * LANES:(hh + 1) * LANES]
        x1 = blk[MLA_NOPE:MLA_NOPE + half]
        x2 = blk[MLA_NOPE + half:MLA_NOPE + MLA_ROPE]
        roped = jnp.concatenate(
            [blk[:MLA_NOPE], x1 * c - x2 * s, x2 * c + x1 * s, blk[MLA_NOPE + MLA_ROPE:]], axis=0)
        q_ref[0, hh] = (roped * scale).astype(BF16)
    v_ref[0, 0] = lax.dot_general(wuvt_ref[...], ckv, nt, preferred_element_type=F32).astype(BF16)


def _mla_proj(h, pos, nw, w_in, q_norm, kv_norm, w_uq, w_ukv, B, S):
    T, D = h.shape
    H = MLA_HEADS
    tm = min(TM_PROJ, S)
    ns = S // tm
    qk = MLA_NOPE + MLA_ROPE
    pad_r = LANES - qk
    w_in_p = jnp.concatenate([
        w_in[:, :MLA_Q_RANK + MLA_KV_RANK],
        jnp.zeros((D, MLA_NOPE), F32), w_in[:, MLA_Q_RANK + MLA_KV_RANK:], jnp.zeros((D, pad_r), F32)],
        axis=1).astype(BF16)
    wq = w_uq.reshape(MLA_Q_RANK, H, qk)
    wq_t = jnp.pad(wq, ((0, 0), (0, 0), (0, pad_r))).reshape(MLA_Q_RANK, H * LANES).T.astype(BF16)
    wkv = w_ukv.reshape(MLA_KV_RANK, H, MLA_NOPE + MLA_V)
    wk_p = jnp.pad(wkv[:, :, :MLA_NOPE], ((0, 0), (0, 0), (0, LANES - MLA_NOPE)))
    wk_p = wk_p.reshape(MLA_KV_RANK, H * LANES).astype(BF16)
    wv_t = wkv[:, :, MLA_NOPE:].reshape(MLA_KV_RANK, H * MLA_V).T.astype(BF16)
    half = MLA_ROPE // 2
    invf = (ROPE_BASE ** (-jnp.arange(half, dtype=F32) / half)).reshape(half, 1)
    wl = w_in_p.shape[1]
    out_shape = (
        jax.ShapeDtypeStruct((B, H, LANES, S), BF16),
        jax.ShapeDtypeStruct((B, H, S, LANES), BF16),
        jax.ShapeDtypeStruct((B, ns, H * MLA_V, tm), BF16),
    )
    scale = qk ** -0.5 * 1.4426950408889634
    return pl.pallas_call(
        functools.partial(_mla_proj_kernel, scale=scale),
        out_shape=out_shape,
        grid=(T // tm,),
        in_specs=[
            pl.BlockSpec((tm, D), lambda i: (i, 0)),
            pl.BlockSpec((1, tm), lambda i: (0, i)),
            _const_spec((1, D)),
            _const_spec((D, wl)),
            _const_spec((1, MLA_Q_RANK)),
            _const_spec((1, MLA_KV_RANK)),
            _const_spec((H * LANES, MLA_Q_RANK)),
            _const_spec((MLA_KV_RANK, H * LANES)),
            _const_spec((H * MLA_V, MLA_KV_RANK)),
            _const_spec((half, 1)),
        ],
        out_specs=(pl.BlockSpec((1, H, LANES, tm), lambda i: (i // ns, 0, 0, i % ns)),
                   pl.BlockSpec((1, H, tm, LANES), lambda i: (i // ns, 0, i % ns, 0)),
                   pl.BlockSpec((1, 1, H * MLA_V, tm), lambda i: (i // ns, i % ns, 0, 0))),
        compiler_params=_params("parallel"),
        name="mla_proj",
    )(h, pos, nw.reshape(1, D), w_in_p, q_norm.reshape(1, -1), kv_norm.reshape(1, -1), wq_t, wk_p, wv_t,
      invf)


def _attn_kernel(q_ref, k_ref, v_ref, o_ref, m_scr, l_scr, acc_scr, s_scr, *, tq):
    qi = pl.program_id(2)
    m_scr[...] = jnp.full(m_scr.shape, -jnp.inf, F32)
    l_scr[...] = jnp.zeros(l_scr.shape, F32)
    acc_scr[...] = jnp.zeros(acc_scr.shape, F32)

    def scores(j, hh):
        return _dot(k_ref[0, hh, pl.ds(pl.multiple_of(j * tq, tq), tq), :], q_ref[0, hh])

    def softmax_pv(s, j, hh, masked):
        if masked:
            kv = lax.broadcasted_iota(I32, s.shape, 0)
            qq = lax.broadcasted_iota(I32, s.shape, 1)
            s = jnp.where(kv <= qq, s, -jnp.inf)
        m_prev = m_scr[hh]
        m_new = jnp.maximum(m_prev, jnp.max(s, axis=0, keepdims=True))
        alpha = jnp.exp2(m_prev - m_new)
        p = jnp.exp2(s - m_new)
        l_scr[hh] = alpha * l_scr[hh] + jnp.sum(p, axis=0, keepdims=True)
        vt = v_ref[0, j, hh * MLA_V:(hh + 1) * MLA_V, :]
        acc_scr[hh] = alpha * acc_scr[hh] + _dot(vt, p.astype(BF16))
        m_scr[hh] = m_new

    def stage(j, slot, masked, prefetch):
        for hh in range(2):
            s = s_scr[slot, hh]
            if prefetch:
                s_scr[1 - slot, hh] = scores(j + 1, hh)
            softmax_pv(s, j, hh, masked)

    for hh in range(2):
        s_scr[0, hh] = scores(0, hh)

    def pair(i, carry):
        stage(2 * i, 0, False, True)
        stage(2 * i + 1, 1, False, True)
        return carry

    lax.fori_loop(0, qi // 2, pair, 0)

    @pl.when(qi % 2 == 1)
    def _():
        stage(qi - 1, 0, False, True)
        stage(qi, 1, True, False)

    @pl.when(qi % 2 == 0)
    def _():
        stage(qi, 0, True, False)
    out = jnp.concatenate([acc_scr[0] / l_scr[0], acc_scr[1] / l_scr[1]], axis=0)
    o_ref[0] = out.T.astype(BF16)


def _attention(qt, k, vt):
    B, H, S, _ = k.shape
    tq = vt.shape[3]
    return pl.pallas_call(
        functools.partial(_attn_kernel, tq=tq),
        out_shape=jax.ShapeDtypeStruct((B, S, H * MLA_V), BF16),
        grid=(B, H // 2, S // tq),
        in_specs=[
            pl.BlockSpec((1, 2, LANES, tq), lambda b, hp, qi: (b, hp, 0, qi)),
            pl.BlockSpec((1, 2, S, LANES), lambda b, hp, qi: (b, hp, 0, 0)),
            pl.BlockSpec((1, S // tq, 2 * MLA_V, tq), lambda b, hp, qi: (b, 0, hp, 0)),
        ],
        out_specs=pl.BlockSpec((1, tq, 2 * MLA_V), lambda b, hp, qi: (b, qi, hp)),
        scratch_shapes=[pltpu.VMEM((2, 1, tq), F32), pltpu.VMEM((2, 1, tq), F32),
                        pltpu.VMEM((2, MLA_V, tq), F32), pltpu.VMEM((2, 2, tq, tq), F32)],
        compiler_params=_params("parallel", "parallel", "arbitrary"),
        name="mla_attention",
    )(qt, k, vt)


def _matmul_res_kernel(x_ref, w_ref, h_ref, o_ref):
    o_ref[...] = h_ref[...] + _dot(x_ref[...], w_ref[...])


def _matmul_residual(x, w, h):
    T, K = x.shape
    N = w.shape[1]
    tm = min(TM_RES, T)
    return pl.pallas_call(
        _matmul_res_kernel,
        out_shape=jax.ShapeDtypeStruct((T, N), F32),
        grid=(T // tm,),
        in_specs=[pl.BlockSpec((tm, K), lambda i: (i, 0)), _const_spec((K, N)),
                  pl.BlockSpec((tm, N), lambda i: (i, 0))],
        out_specs=pl.BlockSpec((tm, N), lambda i: (i, 0)),
        compiler_params=_params("parallel"),
        name="matmul_residual",
    )(x, w.astype(BF16), h)


def _swiglu_steps(x_ref, nw_ref, wg_ref, wu_ref, wd_ref, xn_scr, acc_scr):
    j = pl.program_id(1)

    @pl.when(j == 0)
    def _():
        xn_scr[...] = _rmsnorm(x_ref[...], nw_ref[...]).astype(BF16)
        acc_scr[...] = jnp.zeros(acc_scr.shape, F32)

    xn = xn_scr[...]
    g = _dot(xn, wg_ref[...].astype(BF16))
    u = _dot(xn, wu_ref[...].astype(BF16))
    hmid = (g * jax.nn.sigmoid(g) * u).astype(BF16)
    acc_scr[...] += _dot(hmid, wd_ref[...].astype(BF16))


def _ffn_dense_kernel(x_ref, nw_ref, wg_ref, wu_ref, wd_ref, o_ref, xn_scr, acc_scr):
    _swiglu_steps(x_ref, nw_ref, wg_ref, wu_ref, wd_ref, xn_scr, acc_scr)

    @pl.when(pl.program_id(1) == pl.num_programs(1) - 1)
    def _():
        o_ref[...] = x_ref[...] + acc_scr[...]


def _ffn_dense(h, nw, wg, wu, wd, layer):
    T, D = h.shape
    FF = wg.shape[2]
    tm = min(TM_FFN, T)
    tf = min(TF_FFN, FF)
    return pl.pallas_call(
        _ffn_dense_kernel,
        out_shape=jax.ShapeDtypeStruct((T, D), F32),
        grid=(T // tm, FF // tf),
        in_specs=[
            pl.BlockSpec((tm, D), lambda i, j: (i, 0)),
            _const_spec((1, D)),
            pl.BlockSpec((None, D, tf), lambda i, j: (layer, 0, j)),
            pl.BlockSpec((None, D, tf), lambda i, j: (layer, 0, j)),
            pl.BlockSpec((None, tf, D), lambda i, j: (layer, j, 0)),
        ],
        out_specs=pl.BlockSpec((tm, D), lambda i, j: (i, 0)),
        scratch_shapes=[pltpu.VMEM((tm, D), BF16), pltpu.VMEM((tm, D), F32)],
        compiler_params=_params("parallel", "arbitrary"),
        name="ffn_dense",
    )(h, nw.reshape(1, D), wg, wu, wd)


def _ffn_moe_kernel(te_ref, nv_ref, x_ref, nw_ref, wg_ref, wu_ref, wd_ref, o_ref, xn_scr, acc_scr):
    del te_ref
    valid = pl.program_id(0) < nv_ref[0]
    last = pl.program_id(1) == pl.num_programs(1) - 1

    @pl.when(valid)
    def _():
        _swiglu_steps(x_ref, nw_ref, wg_ref, wu_ref, wd_ref, xn_scr, acc_scr)

        @pl.when(last)
        def _():
            o_ref[...] = acc_scr[...]

    @pl.when(jnp.logical_and(jnp.logical_not(valid), last))
    def _():
        o_ref[...] = jnp.zeros(o_ref.shape, F32)


def _ffn_moe(xs, nw, wg, wu, wd, layer, tile_expert, n_valid, tm):
    P, D = xs.shape
    FF = wg.shape[3]
    tf = min(TF_FFN, FF)
    nj = FF // tf

    def row_map(i, j, te, nv):
        return (jnp.minimum(i, nv[0] - 1), 0)

    def col(i, j, nv):
        return jnp.where(i < nv[0], j, nj - 1)

    def up_map(i, j, te, nv):
        return (layer, te[jnp.minimum(i, nv[0] - 1)], 0, col(i, j, nv))

    def down_map(i, j, te, nv):
        return (layer, te[jnp.minimum(i, nv[0] - 1)], col(i, j, nv), 0)

    grid_spec = pltpu.PrefetchScalarGridSpec(
        num_scalar_prefetch=2,
        grid=(P // tm, nj),
        in_specs=[
            pl.BlockSpec((tm, D), row_map),
            pl.BlockSpec((1, D), lambda i, j, te, nv: (0, 0)),
            pl.BlockSpec((None, None, D, tf), up_map),
            pl.BlockSpec((None, None, D, tf), up_map),
            pl.BlockSpec((None, None, tf, D), down_map),
        ],
        out_specs=pl.BlockSpec((tm, D), lambda i, j, te, nv: (i, 0)),
        scratch_shapes=[pltpu.VMEM((tm, D), BF16), pltpu.VMEM((tm, D), F32)],
    )
    return pl.pallas_call(
        _ffn_moe_kernel,
        out_shape=jax.ShapeDtypeStruct((P, D), F32),
        grid_spec=grid_spec,
        compiler_params=_params("arbitrary", "arbitrary"),
        name="ffn_moe",
    )(tile_expert, n_valid, xs, nw.reshape(1, D), wg, wu, wd)


def _lru_kernel(h_ref, nw_ref, win_ref, cw_ref, cb_ref, wax_ref, ba_ref, bx_ref, lam_ref, wout_ref,
                o_ref, halo_scr, carry_scr, a_scr, b_scr, hs_scr):
    tt, W = h_ref.shape[1], a_scr.shape[2]
    G = tt // SUBLANES

    @pl.when(pl.program_id(1) == 0)
    def _():
        halo_scr[...] = jnp.zeros(halo_scr.shape, F32)
        carry_scr[...] = jnp.zeros(carry_scr.shape, F32)

    x = h_ref[0]
    xn = _rmsnorm(x, nw_ref[...]).astype(BF16)
    z = _dot(xn, win_ref[...])
    gate = jax.nn.gelu(z[:, :W])
    xb = z[:, W:]
    xe = jnp.concatenate([halo_scr[...], xb], axis=0)
    halo_scr[...] = xb[tt - SUBLANES:, :]
    cw = cw_ref[...]
    xc = cb_ref[...]
    for kk in range(CONV_WIDTH):
        off = SUBLANES - (CONV_WIDTH - 1) + kk
        xc = xc + cw[kk:kk + 1, :] * xe[off:off + tt, :]
    xcb = xc.astype(BF16)
    hd = W // LRU_HEADS
    r_parts, i_parts = [], []
    for n in range(LRU_HEADS):
        ri = _dot(xcb[:, n * hd:(n + 1) * hd], wax_ref[n])
        r_parts.append(ri[:, :hd])
        i_parts.append(ri[:, hd:])
    r = jax.nn.sigmoid(jnp.concatenate(r_parts, axis=1) + ba_ref[...])
    ig = jax.nn.sigmoid(jnp.concatenate(i_parts, axis=1) + bx_ref[...])
    nl = -lam_ref[...]
    softplus = jnp.maximum(nl, 0.0) + jnp.log1p(jnp.exp(-jnp.abs(nl)))
    a = jnp.exp(-LRU_C * r * softplus)
    b = jnp.sqrt(1.0 - a * a) * (ig * xc)

    A = a.reshape(G, SUBLANES, W)
    Bv = b.reshape(G, SUBLANES, W)
    sub = lax.broadcasted_iota(I32, A.shape, 1)
    for d in (1, 2, 4):
        keep = sub >= d
        a_sh = jnp.where(keep, pltpu.roll(A, d, 1), 1.0)
        b_sh = jnp.where(keep, pltpu.roll(Bv, d, 1), 0.0)
        Bv = A * b_sh + Bv
        A = A * a_sh
    a_scr[...] = A
    b_scr[...] = Bv

    def body(g, hprev):
        hg = b_scr[g] + a_scr[g] * hprev
        hs_scr[g] = hg
        return hg[SUBLANES - 1:, :]

    carry_scr[...] = lax.fori_loop(0, G, body, carry_scr[...], unroll=8)
    y = (hs_scr[...].reshape(tt, W) * gate).astype(BF16)
    o_ref[0] = x + _dot(y, wout_ref[...])


def _lru_mixer(h, nw, w_in, conv_w, conv_b, w_a, b_a, w_x, b_x, lam, w_out):
    B, S, D = h.shape
    W = w_out.shape[0]
    tt = min(TT_LRU, S)
    hd = W // LRU_HEADS
    wax = jnp.concatenate([w_a, w_x], axis=2).astype(BF16)
    G = tt // SUBLANES
    return pl.pallas_call(
        _lru_kernel,
        out_shape=jax.ShapeDtypeStruct((B, S, D), F32),
        grid=(B, S // tt),
        in_specs=[
            pl.BlockSpec((1, tt, D), lambda b, t: (b, t, 0)),
            _const_spec((1, D)),
            _const_spec((D, 2 * W)),
            _const_spec((CONV_WIDTH, W)),
            _const_spec((1, W)),
            _const_spec((LRU_HEADS, hd, 2 * hd)),
            _const_spec((1, W)),
            _const_spec((1, W)),
            _const_spec((1, W)),
            _const_spec((W, D)),
        ],
        out_specs=pl.BlockSpec((1, tt, D), lambda b, t: (b, t, 0)),
        scratch_shapes=[
            pltpu.VMEM((SUBLANES, W), F32), pltpu.VMEM((1, W), F32),
            pltpu.VMEM((G, SUBLANES, W), F32), pltpu.VMEM((G, SUBLANES, W), F32),
            pltpu.VMEM((G, SUBLANES, W), F32),
        ],
        compiler_params=_params("parallel", "arbitrary"),
        name="rglru_mixer",
    )(h, nw.reshape(1, D), w_in.astype(BF16), conv_w, conv_b.reshape(1, W), wax, b_a.reshape(1, W),
      b_x.reshape(1, W), lam.reshape(1, W), w_out.astype(BF16))


def _pool_kernel(h_ref, halo_ref, nw_ref, wg_ref, sc_ref, o_ref, *, blocks_per_seq):
    tm, D = h_ref.shape
    gsz = D // len(POOL_WINDOWS)
    x = h_ref[...]
    first = pl.program_id(0) % blocks_per_seq == 0
    xn = _rmsnorm(x, nw_ref[...])
    hn_halo = _rmsnorm(halo_ref[...], nw_ref[...])
    hn_halo = jnp.where(first, 0.0, hn_halo)
    e = jnp.concatenate([hn_halo, xn], axis=0)
    t = (pl.program_id(0) % blocks_per_seq) * tm + lax.broadcasted_iota(I32, (tm, 1), 0)
    outs = []
    for g, w in enumerate(POOL_WINDOWS):
        cur = e[:, g * gsz:(g + 1) * gsz]
        base = 0
        d = 1
        while d < w:
            cur = cur[d:, :] + cur[:-d, :]
            base += d
            d *= 2
        win = cur[POOL_HALO - base:POOL_HALO - base + tm, :]
        count = jnp.minimum(t + 1, w).astype(F32)
        pooled = (win / count - xn[:, g * gsz:(g + 1) * gsz]).astype(BF16)
        outs.append(_dot(pooled, wg_ref[g]))
    o_ref[...] = x + jnp.concatenate(outs, axis=1) * sc_ref[...]


def _pool_mixer(h, nw, w_grp, scale, S):
    T, D = h.shape
    tm = min(TM_POOL, S)
    bps = S // tm
    hb = tm // POOL_HALO
    ng, gsz = w_grp.shape[0], w_grp.shape[1]
    return pl.pallas_call(
        functools.partial(_pool_kernel, blocks_per_seq=bps),
        out_shape=jax.ShapeDtypeStruct((T, D), F32),
        grid=(T // tm,),
        in_specs=[
            pl.BlockSpec((tm, D), lambda i: (i, 0)),
            pl.BlockSpec((POOL_HALO, D), lambda i: (jnp.maximum(i * hb - 1, 0), 0)),
            _const_spec((1, D)),
            _const_spec((ng, gsz, gsz)),
            _const_spec((1, D)),
        ],
        out_specs=pl.BlockSpec((tm, D), lambda i: (i, 0)),
        compiler_params=_params("parallel"),
        name="pool_mixer",
    )(h, h, nw.reshape(1, D), w_grp.astype(BF16), scale.reshape(1, D))


def _gmlp_kernel(h_ref, nw_ref, win_ref, lg_ref, lb_ref, ws_ref, bs_ref, wout_ref, o_ref):
    tm, D = h_ref.shape
    W = wout_ref.shape[0]
    gw = W // GMLP_GROUPS
    x = h_ref[...]
    xn = _rmsnorm(x, nw_ref[...]).astype(BF16)
    z = jax.nn.gelu(_dot(xn, win_ref[...]))
    u = z[:, :W]
    v = z[:, W:]
    mu = jnp.mean(v, axis=-1, keepdims=True)
    vc = v - mu
    var = jnp.mean(vc * vc, axis=-1, keepdims=True)
    vb = (vc * lax.rsqrt(var + LN_EPS) * lg_ref[...] + lb_ref[...]).astype(BF16)
    row = lax.broadcasted_iota(I32, (GMLP_CHUNK, GMLP_CHUNK), 0)
    col = lax.broadcasted_iota(I32, (GMLP_CHUNK, GMLP_CHUNK), 1)
    bs = bs_ref[...]
    ws = [jnp.where(col <= row, ws_ref[g], 0.0).astype(BF16) for g in range(GMLP_GROUPS)]
    rows = []
    for c in range(tm // GMLP_CHUNK):
        cols = []
        for g in range(GMLP_GROUPS):
            blk = vb[c * GMLP_CHUNK:(c + 1) * GMLP_CHUNK, g * gw:(g + 1) * gw]
            cols.append(_dot(ws[g], blk) + bs[:, g:g + 1])
        rows.append(jnp.concatenate(cols, axis=1))
    v2 = jnp.concatenate(rows, axis=0)
    o_ref[...] = x + _dot((u * v2).astype(BF16), wout_ref[...])


def _gmlp_mixer(h, nw, w_in, ln_g, ln_b, w_s, b_s, w_out):
    T, D = h.shape
    W = w_out.shape[0]
    tm = min(TM_GMLP, T)
    return pl.pallas_call(
        _gmlp_kernel,
        out_shape=jax.ShapeDtypeStruct((T, D), F32),
        grid=(T // tm,),
        in_specs=[
            pl.BlockSpec((tm, D), lambda i: (i, 0)),
            _const_spec((1, D)),
            _const_spec((D, 2 * W)),
            _const_spec((1, W)),
            _const_spec((1, W)),
            _const_spec((GMLP_GROUPS, GMLP_CHUNK, GMLP_CHUNK)),
            _const_spec((GMLP_CHUNK, GMLP_GROUPS)),
            _const_spec((W, D)),
        ],
        out_specs=pl.BlockSpec((tm, D), lambda i: (i, 0)),
        compiler_params=_params("parallel"),
        name="gmlp_mixer",
    )(h, nw.reshape(1, D), w_in.astype(BF16), ln_g.reshape(1, W), ln_b.reshape(1, W), w_s, b_s.T,
      w_out.astype(BF16))


def _router_kernel(h_ref, nw_ref, wr_ref, meta_ref, gate_ref, cnt_ref, cnt_scr):
    tm = h_ref.shape[0]

    @pl.when(pl.program_id(0) == 0)
    def _():
        cnt_scr[...] = jnp.zeros(cnt_scr.shape, F32)

    xn = _rmsnorm(h_ref[...], nw_ref[...])
    logits = jnp.dot(xn, wr_ref[...], precision=lax.Precision.HIGHEST, preferred_element_type=F32)
    lane = lax.broadcasted_iota(I32, logits.shape, 1)
    logits = jnp.where(lane < N_EXPERTS, logits, -jnp.inf)
    m1 = jnp.max(logits, axis=1, keepdims=True)
    e1 = jnp.min(jnp.where(logits == m1, lane, LANES), axis=1, keepdims=True)
    rest = jnp.where(lane == e1, -jnp.inf, logits)
    m2 = jnp.max(rest, axis=1, keepdims=True)
    e2 = jnp.min(jnp.where(rest == m2, lane, LANES), axis=1, keepdims=True)
    ex = jnp.exp(m2 - m1)
    g1 = 1.0 / (1.0 + ex)
    g2 = ex / (1.0 + ex)
    hit1 = lane == e1
    hit2 = lane == e2
    onehot = jnp.where(hit1 | hit2, 1.0, 0.0)
    row = lax.broadcasted_iota(I32, (tm, tm), 0)
    col = lax.broadcasted_iota(I32, (tm, tm), 1)
    tri = jnp.where(col < row, 1.0, 0.0).astype(BF16)
    before = _dot(tri, onehot.astype(BF16)) + cnt_scr[...]
    r1 = jnp.sum(jnp.where(hit1, before, 0.0), axis=1, keepdims=True).astype(I32)
    r2 = jnp.sum(jnp.where(hit2, before, 0.0), axis=1, keepdims=True).astype(I32)
    cnt_scr[...] += jnp.sum(onehot, axis=0, keepdims=True)
    meta = jnp.where(lane == 0, e1, jnp.where(lane == 1, e2, jnp.where(lane == 2, r1, r2)))
    meta_ref[...] = meta
    gate_ref[...] = jnp.where(lane == 0, g1, g2)
    cnt_ref[...] = cnt_scr[...]


def _router(h, nw, w_router):
    T, D = h.shape
    tm = min(TM_ROUTE, T)
    wr = jnp.pad(w_router, ((0, 0), (0, LANES - N_EXPERTS)))
    return pl.pallas_call(
        _router_kernel,
        out_shape=(jax.ShapeDtypeStruct((T, LANES), I32), jax.ShapeDtypeStruct((T, LANES), F32),
                   jax.ShapeDtypeStruct((1, LANES), F32)),
        grid=(T // tm,),
        in_specs=[pl.BlockSpec((tm, D), lambda i: (i, 0)), _const_spec((1, D)), _const_spec((D, LANES))],
        out_specs=(pl.BlockSpec((tm, LANES), lambda i: (i, 0)), pl.BlockSpec((tm, LANES), lambda i: (i, 0)),
                   _const_spec((1, LANES))),
        scratch_shapes=[pltpu.VMEM((1, LANES), F32)],
        compiler_params=_params("arbitrary"),
        name="moe_router",
    )(h, nw.reshape(1, D), wr)


def _dispatch_kernel(d1_ref, d2_ref, h_ref, xs_in_ref, xs_ref, sem):
    del xs_in_ref
    tm = h_ref.shape[0]

    def copies(r):
        src = h_ref.at[pl.ds(r, 1), :]
        return (pltpu.make_async_copy(src, xs_ref.at[pl.ds(d1_ref[0, 0, r], 1), :], sem),
                pltpu.make_async_copy(src, xs_ref.at[pl.ds(d2_ref[0, 0, r], 1), :], sem))

    def start(r, c):
        for cp in copies(r):
            cp.start()
        return c

    def wait(r, c):
        for cp in copies(r):
            cp.wait()
        return c

    lax.fori_loop(0, tm, start, 0)
    lax.fori_loop(0, tm, wait, 0)


def _dispatch(h, d1, d2, P):
    T, D = h.shape
    tm = min(TM_DISPATCH, T)
    nb = T // tm
    smem = functools.partial(pl.BlockSpec, memory_space=pltpu.SMEM)
    return pl.pallas_call(
        _dispatch_kernel,
        out_shape=jax.ShapeDtypeStruct((P, D), F32),
        grid=(nb,),
        in_specs=[
            smem((1, 1, tm), lambda i: (i, 0, 0)),
            smem((1, 1, tm), lambda i: (i, 0, 0)),
            pl.BlockSpec((tm, D), lambda i: (i, 0)),
            pl.BlockSpec(memory_space=pl.ANY),
        ],
        out_specs=pl.BlockSpec(memory_space=pl.ANY),
        scratch_shapes=[pltpu.SemaphoreType.DMA],
        input_output_aliases={3: 0},
        compiler_params=_params("arbitrary"),
        name="moe_dispatch",
    )(d1.reshape(nb, 1, tm), d2.reshape(nb, 1, tm), h, jnp.zeros((P, D), F32))


def _combine_kernel(d1_ref, d2_ref, h_ref, gate_ref, ys_ref, *rest, final_norm):
    if final_norm:
        fw_ref, o_ref, buf1, buf2, sem = rest
    else:
        o_ref, buf1, buf2, sem = rest
    tm = h_ref.shape[0]

    def copies(r):
        return (pltpu.make_async_copy(ys_ref.at[pl.ds(d1_ref[0, 0, r], 1), :], buf1.at[pl.ds(r, 1), :], sem),
                pltpu.make_async_copy(ys_ref.at[pl.ds(d2_ref[0, 0, r], 1), :], buf2.at[pl.ds(r, 1), :], sem))

    def start(r, c):
        for cp in copies(r):
            cp.start()
        return c

    def wait(r, c):
        for cp in copies(r):
            cp.wait()
        return c

    lax.fori_loop(0, tm, start, 0)
    lax.fori_loop(0, tm, wait, 0)
    gates = gate_ref[...]
    out = h_ref[...] + gates[:, 0:1] * buf1[...] + gates[:, 1:2] * buf2[...]
    if final_norm:
        out = _rmsnorm(out, fw_ref[...])
    o_ref[...] = out


def _combine(h, gates, ys, d1, d2, final_w=None):
    T, D = h.shape
    tm = min(TM_COMBINE, T)
    nb = T // tm
    smem = functools.partial(pl.BlockSpec, memory_space=pltpu.SMEM)
    in_specs = [
        smem((1, 1, tm), lambda i: (i, 0, 0)),
        smem((1, 1, tm), lambda i: (i, 0, 0)),
        pl.BlockSpec((tm, D), lambda i: (i, 0)),
        pl.BlockSpec((tm, LANES), lambda i: (i, 0)),
        pl.BlockSpec(memory_space=pl.ANY),
    ]
    args = [d1.reshape(nb, 1, tm), d2.reshape(nb, 1, tm), h, gates, ys]
    if final_w is not None:
        in_specs.append(_const_spec((1, D)))
        args.append(final_w.reshape(1, D))
    return pl.pallas_call(
        functools.partial(_combine_kernel, final_norm=final_w is not None),
        out_shape=jax.ShapeDtypeStruct((T, D), F32),
        grid=(nb,),
        in_specs=in_specs,
        out_specs=pl.BlockSpec((tm, D), lambda i: (i, 0)),
        scratch_shapes=[pltpu.VMEM((tm, D), F32), pltpu.VMEM((tm, D), F32), pltpu.SemaphoreType.DMA],
        compiler_params=_params("arbitrary"),
        name="moe_combine",
    )(*args)


def _moe_layer(h, nw, w_router, wg, wu, wd, layer, final_w=None):
    T, D = h.shape
    meta, gates, counts = _router(h, nw, w_router)
    tm = min(TM_MOE, T)
    sizes = counts[0, :N_EXPERTS].astype(I32)
    padded = (sizes + tm - 1) // tm * tm
    pad_end = jnp.cumsum(padded)
    pad_start = pad_end - padded
    d1 = pad_start[meta[:, 0]] + meta[:, 2]
    d2 = pad_start[meta[:, 1]] + meta[:, 3]
    n_tiles = (T * TOP_K + N_EXPERTS * (tm - 1)) // tm
    tile_start = jnp.arange(n_tiles, dtype=I32) * tm
    tile_expert = jnp.minimum(jnp.sum(pad_end[None, :] <= tile_start[:, None], axis=1), N_EXPERTS - 1)
    n_valid = (pad_end[-1] // tm).reshape(1)
    xs = _dispatch(h, d1, d2, n_tiles * tm)
    ys = _ffn_moe(xs, nw, wg, wu, wd, layer, tile_expert.astype(I32), n_valid.astype(I32), tm)
    return _combine(h, gates, ys, d1, d2, final_w)


def kernel(x, positions, norm_mix, norm_ffn, norm_final, mla_w_in, mla_q_norm, mla_kv_norm, mla_w_uq,
           mla_w_ukv, mla_w_o, lru_w_in, lru_conv_w, lru_conv_b, lru_w_a, lru_b_a, lru_w_x, lru_b_x,
           lru_lam, lru_w_out, pool_w_grp, pool_scale, gmlp_w_in, gmlp_ln_g, gmlp_ln_b, gmlp_w_s,
           gmlp_b_s, gmlp_w_out, ffn_w_gate, ffn_w_up, ffn_w_down, moe_w_router, moe_w_gate, moe_w_up,
           moe_w_down):
    B, S, D = x.shape
    T = B * S
    depth = norm_mix.shape[0]
    h = x.reshape(T, D)
    for i in range(depth):
        m, j = i % 4, i // 4
        if m == 0:
            q, k, v = _mla_proj(h, positions.reshape(1, T), norm_mix[i], mla_w_in[j], mla_q_norm[j],
                                mla_kv_norm[j], mla_w_uq[j], mla_w_ukv[j], B, S)
            o = _attention(q, k, v)
            h = _matmul_residual(o.reshape(T, -1), mla_w_o[j], h)
        elif m == 1:
            h = _lru_mixer(h.reshape(B, S, D), norm_mix[i], lru_w_in[j], lru_conv_w[j], lru_conv_b[j],
                           lru_w_a[j], lru_b_a[j], lru_w_x[j], lru_b_x[j], lru_lam[j],
                           lru_w_out[j]).reshape(T, D)
        elif m == 2:
            h = _pool_mixer(h, norm_mix[i], pool_w_grp[j], pool_scale[j], S)
        else:
            h = _gmlp_mixer(h, norm_mix[i], gmlp_w_in[j], gmlp_ln_g[j], gmlp_ln_b[j], gmlp_w_s[j],
                            gmlp_b_s[j], gmlp_w_out[j])
        kk = i // 2
        if i % 2 == 0:
            h = _ffn_dense(h, norm_ffn[i], ffn_w_gate, ffn_w_up, ffn_w_down, kk)
        else:
            last = i == depth - 1
            h = _moe_layer(h, norm_ffn[i], moe_w_router[kk], moe_w_gate, moe_w_up, moe_w_down, kk,
                           norm_final if last else None)
    if depth % 2 == 1:
        raise NotImplementedError("final RMSNorm is fused into the last MoE combine")
    return h.reshape(B, S, D)
```

```python
import functools

import jax
import jax.numpy as jnp
from jax import lax
from jax.experimental import pallas as pl
from jax.experimental.pallas import tpu as pltpu

F32 = jnp.float32
BF16 = jnp.bfloat16
I32 = jnp.int32

RMS_EPS = 1e-6
LN_EPS = 1e-5

MLA_HEADS = 16
MLA_Q_RANK = 384
MLA_KV_RANK = 256
MLA_NOPE = 64
MLA_ROPE = 32
MLA_V = 64
ROPE_BASE = 10000.0
LRU_HEADS = 8
CONV_WIDTH = 4
LRU_C = 8.0
POOL_WINDOWS = (2, 4, 8, 16)
GMLP_GROUPS = 4
GMLP_CHUNK = 128
N_EXPERTS = 8
TOP_K = 2

LANES = 128
SUBLANES = 8
VMEM_LIMIT_BYTES = 56 * 1024 * 1024

TM_PROJ = 512
TQ_ATTN = 512
TM_RES = 512
TM_FFN = 1024
TF_FFN = 512
TT_LRU = 512
TM_POOL = 512
TM_GMLP = 512
TM_ROUTE = 512
TM_MOE = 1024
TM_DISPATCH = 512
TM_COMBINE = 256
POOL_HALO = 16


def _params(*sem):
    return pltpu.CompilerParams(dimension_semantics=sem, vmem_limit_bytes=VMEM_LIMIT_BYTES)


def _rmsnorm(x, g):
    return x * lax.rsqrt(jnp.mean(x * x, axis=-1, keepdims=True) + RMS_EPS) * g


def _dot(a, b):
    return jnp.dot(a, b, preferred_element_type=F32)


def _const_spec(shape):
    return pl.BlockSpec(shape, lambda *_: (0,) * len(shape))


def _mla_proj_kernel(h_ref, pos_ref, nw_ref, win_ref, qn_ref, kvn_ref, wuqt_ref, wuk_ref, wuvt_ref,
                     invf_ref, q_ref, k_ref, v_ref, *, scale):
    half = MLA_ROPE // 2
    x = h_ref[...]
    tm = x.shape[0]
    xn = _rmsnorm(x, nw_ref[...]).astype(BF16)
    lat = _dot(xn, win_ref[...])
    cq = _rmsnorm(lat[:, :MLA_Q_RANK], qn_ref[...]).astype(BF16)
    ckv = _rmsnorm(lat[:, MLA_Q_RANK:MLA_Q_RANK + MLA_KV_RANK], kvn_ref[...]).astype(BF16)
    kr = lat[:, MLA_Q_RANK + MLA_KV_RANK:]
    ang = invf_ref[...] * pos_ref[...].astype(F32)
    c = jnp.cos(ang)
    s = jnp.sin(ang)
    pad_r = LANES - MLA_NOPE - MLA_ROPE
    one = jnp.ones((MLA_NOPE, tm), F32)
    c_t = jnp.concatenate([one, c, c, jnp.ones((pad_r, tm), F32)], axis=0).T
    s_lo = jnp.concatenate([0.0 * one, -s, jnp.zeros((half + pad_r, tm), F32)], axis=0).T
    s_hi = jnp.concatenate([0.0 * one, 0.0 * s, s, jnp.zeros((pad_r, tm), F32)], axis=0).T
    kr = kr * c_t + pltpu.roll(kr, LANES - half, 1) * s_lo + pltpu.roll(kr, half, 1) * s_hi
    kn = _dot(ckv, wuk_ref[...])
    for hh in range(MLA_HEADS):
        k_ref[0, hh] = (kn[:, hh * LANES:(hh + 1) * LANES] + kr).astype(BF16)
    nt = (((1,), (1,)), ((), ()))
    qt = lax.dot_general(wuqt_ref[...], cq, nt, preferred_element_type=F32)
    for hh in range(MLA_HEADS):
        blk = qt[hh * LANES:(hh + 1) * LANES]
        x1 = blk[MLA_NOPE:MLA_NOPE + half]
        x2 = blk[MLA_NOPE + half:MLA_NOPE + MLA_ROPE]
        roped = jnp.concatenate(
            [blk[:MLA_NOPE], x1 * c - x2 * s, x2 * c + x1 * s, blk[MLA_NOPE + MLA_ROPE:]], axis=0)
        q_ref[0, hh] = (roped * scale).astype(BF16)
    v_ref[0, 0] = lax.dot_general(wuvt_ref[...], ckv, nt, preferred_element_type=F32).astype(BF16)


def _mla_proj(h, pos, nw, w_in, q_norm, kv_norm, w_uq, w_ukv, B, S):
    T, D = h.shape
    H = MLA_HEADS
    tm = min(TM_PROJ, S)
    ns = S // tm
    qk = MLA_NOPE + MLA_ROPE
    pad_r = LANES - qk
    w_in_p = jnp.concatenate([
        w_in[:, :MLA_Q_RANK + MLA_KV_RANK],
        jnp.zeros((D, MLA_NOPE), F32), w_in[:, MLA_Q_RANK + MLA_KV_RANK:], jnp.zeros((D, pad_r), F32)],
        axis=1).astype(BF16)
    wq = w_uq.reshape(MLA_Q_RANK, H, qk)
    wq_t = jnp.pad(wq, ((0, 0), (0, 0), (0, pad_r))).reshape(MLA_Q_RANK, H * LANES).T.astype(BF16)
    wkv = w_ukv.reshape(MLA_KV_RANK, H, MLA_NOPE + MLA_V)
    wk_p = jnp.pad(wkv[:, :, :MLA_NOPE], ((0, 0), (0, 0), (0, LANES - MLA_NOPE)))
    wk_p = wk_p.reshape(MLA_KV_RANK, H * LANES).astype(BF16)
    wv_t = wkv[:, :, MLA_NOPE:].reshape(MLA_KV_RANK, H * MLA_V).T.astype(BF16)
    half = MLA_ROPE // 2
    invf = (ROPE_BASE ** (-jnp.arange(half, dtype=F32) / half)).reshape(half, 1)
    wl = w_in_p.shape[1]
    out_shape = (
        jax.ShapeDtypeStruct((B, H, LANES, S), BF16),
        jax.ShapeDtypeStruct((B, H, S, LANES), BF16),
        jax.ShapeDtypeStruct((B, ns, H * MLA_V, tm), BF16),
    )
    scale = qk ** -0.5 * 1.4426950408889634
    return pl.pallas_call(
        functools.partial(_mla_proj_kernel, scale=scale),
        out_shape=out_shape,
        grid=(T // tm,),
        in_specs=[
            pl.BlockSpec((tm, D), lambda i: (i, 0)),
            pl.BlockSpec((1, tm), lambda i: (0, i)),
            _const_spec((1, D)),
            _const_spec((D, wl)),
            _const_spec((1, MLA_Q_RANK)),
            _const_spec((1, MLA_KV_RANK)),
            _const_spec((H * LANES, MLA_Q_RANK)),
            _const_spec((MLA_KV_RANK, H * LANES)),
            _const_spec((H * MLA_V, MLA_KV_RANK)),
            _const_spec((half, 1)),
        ],
        out_specs=(pl.BlockSpec((1, H, LANES, tm), lambda i: (i // ns, 0, 0, i % ns)),
                   pl.BlockSpec((1, H, tm, LANES), lambda i: (i // ns, 0, i % ns, 0)),
                   pl.BlockSpec((1, 1, H * MLA_V, tm), lambda i: (i // ns, i % ns, 0, 0))),
        compiler_params=_params("parallel"),
        name="mla_proj",
    )(h, pos, nw.reshape(1, D), w_in_p, q_norm.reshape(1, -1), kv_norm.reshape(1, -1), wq_t, wk_p, wv_t,
      invf)


def _attn_kernel(q_ref, k_ref, v_ref, o_ref, m_scr, l_scr, acc_scr, s_scr, *, tq):
    qi = pl.program_id(2)
    m_scr[...] = jnp.full(m_scr.shape, -jnp.inf, F32)
    l_scr[...] = jnp.zeros(l_scr.shape, F32)
    acc_scr[...] = jnp.zeros(acc_scr.shape, F32)

    def scores(j, hh):
        return _dot(k_ref[0, hh, pl.ds(pl.multiple_of(j * tq, tq), tq), :], q_ref[0, hh])

    def softmax_pv(s, j, hh, masked):
        if masked:
            kv = lax.broadcasted_iota(I32, s.shape, 0)
            qq = lax.broadcasted_iota(I32, s.shape, 1)
            s = jnp.where(kv <= qq, s, -jnp.inf)
        m_prev = m_scr[hh]
        m_new = jnp.maximum(m_prev, jnp.max(s, axis=0, keepdims=True))
        alpha = jnp.exp2(m_prev - m_new)
        p = jnp.exp2(s - m_new)
        l_scr[hh] = alpha * l_scr[hh] + jnp.sum(p, axis=0, keepdims=True)
        vt = v_ref[0, j, hh * MLA_V:(hh + 1) * MLA_V, :]
        acc_scr[hh] = alpha * acc_scr[hh] + _dot(vt, p.astype(BF16))
        m_scr[hh] = m_new

    def stage(j, slot, masked, prefetch):
        for hh in range(2):
            s = s_scr[slot, hh]
            if prefetch:
                s_scr[1 - slot, hh] = scores(j + 1, hh)
            softmax_pv(s, j, hh, masked)

    for hh in range(2):
        s_scr[0, hh] = scores(0, hh)

    def pair(i, carry):
        stage(2 * i, 0, False, True)
        stage(2 * i + 1, 1, False, True)
        return carry

    lax.fori_loop(0, qi // 2, pair, 0)

    @pl.when(qi % 2 == 1)
    def _():
        stage(qi - 1, 0, False, True)
        stage(qi, 1, True, False)

    @pl.when(qi % 2 == 0)
    def _():
        stage(qi, 0, True, False)
    out = jnp.concatenate([acc_scr[0] / l_scr[0], acc_scr[1] / l_scr[1]], axis=0)
    o_ref[0] = out.T.astype(BF16)


def _attention(qt, k, vt):
    B, H, S, _ = k.shape
    tq = vt.shape[3]
    return pl.pallas_call(
        functools.partial(_attn_kernel, tq=tq),
        out_shape=jax.ShapeDtypeStruct((B, S, H * MLA_V), BF16),
        grid=(B, H // 2, S // tq),
        in_specs=[
            pl.BlockSpec((1, 2, LANES, tq), lambda b, hp, qi: (b, hp, 0, qi)),
            pl.BlockSpec((1, 2, S, LANES), lambda b, hp, qi: (b, hp, 0, 0)),
            pl.BlockSpec((1, S // tq, 2 * MLA_V, tq), lambda b, hp, qi: (b, 0, hp, 0)),
        ],
        out_specs=pl.BlockSpec((1, tq, 2 * MLA_V), lambda b, hp, qi: (b, qi, hp)),
        scratch_shapes=[pltpu.VMEM((2, 1, tq), F32), pltpu.VMEM((2, 1, tq), F32),
                        pltpu.VMEM((2, MLA_V, tq), F32), pltpu.VMEM((2, 2, tq, tq), F32)],
        compiler_params=_params("parallel", "parallel", "arbitrary"),
        name="mla_attention",
    )(qt, k, vt)


def _matmul_res_kernel(x_ref, w_ref, h_ref, o_ref):
    o_ref[...] = h_ref[...] + _dot(x_ref[...], w_ref[...])


def _matmul_residual(x, w, h):
    T, K = x.shape
    N = w.shape[1]
    tm = min(TM_RES, T)
    return pl.pallas_call(
        _matmul_res_kernel,
        out_shape=jax.ShapeDtypeStruct((T, N), F32),
        grid=(T // tm,),
        in_specs=[pl.BlockSpec((tm, K), lambda i: (i, 0)), _const_spec((K, N)),
                  pl.BlockSpec((tm, N), lambda i: (i, 0))],
        out_specs=pl.BlockSpec((tm, N), lambda i: (i, 0)),
        compiler_params=_params("parallel"),
        name="matmul_residual",
    )(x, w.astype(BF16), h)


def _swiglu_steps(load_x, nw_ref, wg_ref, wu_ref, wd_ref, xn_scr, acc_scr):
    j = pl.program_id(1)

    @pl.when(j == 0)
    def _():
        xn_scr[...] = _rmsnorm(load_x(), nw_ref[...]).astype(BF16)
        acc_scr[...] = jnp.zeros(acc_scr.shape, F32)

    xn = xn_scr[...]
    g = _dot(xn, wg_ref[...].astype(BF16))
    u = _dot(xn, wu_ref[...].astype(BF16))
    hmid = (g * jax.nn.sigmoid(g) * u).astype(BF16)
    acc_scr[...] += _dot(hmid, wd_ref[...].astype(BF16))


def _ffn_dense_kernel(x_ref, nw_ref, wg_ref, wu_ref, wd_ref, o_ref, xn_scr, acc_scr):
    _swiglu_steps(lambda: x_ref[...], nw_ref, wg_ref, wu_ref, wd_ref, xn_scr, acc_scr)

    @pl.when(pl.program_id(1) == pl.num_programs(1) - 1)
    def _():
        o_ref[...] = x_ref[...] + acc_scr[...]


def _ffn_dense(h, nw, wg, wu, wd, layer):
    T, D = h.shape
    FF = wg.shape[2]
    tm = min(TM_FFN, T)
    tf = min(TF_FFN, FF)
    return pl.pallas_call(
        _ffn_dense_kernel,
        out_shape=jax.ShapeDtypeStruct((T, D), F32),
        grid=(T // tm, FF // tf),
        in_specs=[
            pl.BlockSpec((tm, D), lambda i, j: (i, 0)),
            _const_spec((1, D)),
            pl.BlockSpec((None, D, tf), lambda i, j: (layer, 0, j)),
            pl.BlockSpec((None, D, tf), lambda i, j: (layer, 0, j)),
            pl.BlockSpec((None, tf, D), lambda i, j: (layer, j, 0)),
        ],
        out_specs=pl.BlockSpec((tm, D), lambda i, j: (i, 0)),
        scratch_shapes=[pltpu.VMEM((tm, D), BF16), pltpu.VMEM((tm, D), F32)],
        compiler_params=_params("parallel", "arbitrary"),
        name="ffn_dense",
    )(h, nw.reshape(1, D), wg, wu, wd)


def _rows_to_tile(ref, tm):
    return jnp.concatenate([ref[pl.ds(s, tm, stride=SUBLANES), :] for s in range(SUBLANES)], axis=1)


def _tile_to_rows(ref, x, tm):
    for s in range(SUBLANES):
        ref[pl.ds(s, tm, stride=SUBLANES), :] = x[:, s * LANES:(s + 1) * LANES]


def _ffn_moe_kernel(te_ref, nv_ref, x_ref, nw_ref, wg_ref, wu_ref, wd_ref, o_ref, xn_scr, acc_scr):
    del te_ref
    tm = xn_scr.shape[0]
    valid = pl.program_id(0) < nv_ref[0]
    last = pl.program_id(1) == pl.num_programs(1) - 1

    @pl.when(valid)
    def _():
        _swiglu_steps(lambda: _rows_to_tile(x_ref, tm), nw_ref, wg_ref, wu_ref, wd_ref, xn_scr, acc_scr)

        @pl.when(last)
        def _():
            _tile_to_rows(o_ref, acc_scr[...], tm)

    @pl.when(jnp.logical_and(jnp.logical_not(valid), last))
    def _():
        o_ref[...] = jnp.zeros(o_ref.shape, F32)


def _ffn_moe(xs, nw, wg, wu, wd, layer, tile_expert, n_valid, tm):
    D = wg.shape[2]
    FF = wg.shape[3]
    P = xs.shape[0] // SUBLANES
    tf = min(TF_FFN, FF)
    nj = FF // tf

    def tile(i, nv):
        return jnp.maximum(jnp.minimum(i, nv[0] - 1), 0)

    def col(i, j, nv):
        return jnp.where(i < nv[0], j, nj - 1)

    def up_map(i, j, te, nv):
        return (layer, te[tile(i, nv)], 0, col(i, j, nv))

    def down_map(i, j, te, nv):
        return (layer, te[tile(i, nv)], col(i, j, nv), 0)

    grid_spec = pltpu.PrefetchScalarGridSpec(
        num_scalar_prefetch=2,
        grid=(P // tm, nj),
        in_specs=[
            pl.BlockSpec((tm * SUBLANES, LANES), lambda i, j, te, nv: (tile(i, nv), 0)),
            pl.BlockSpec((1, D), lambda i, j, te, nv: (0, 0)),
            pl.BlockSpec((None, None, D, tf), up_map),
            pl.BlockSpec((None, None, D, tf), up_map),
            pl.BlockSpec((None, None, tf, D), down_map),
        ],
        out_specs=pl.BlockSpec((tm * SUBLANES, LANES), lambda i, j, te, nv: (i, 0)),
        scratch_shapes=[pltpu.VMEM((tm, D), BF16), pltpu.VMEM((tm, D), F32)],
    )
    return pl.pallas_call(
        _ffn_moe_kernel,
        out_shape=jax.ShapeDtypeStruct((P * SUBLANES, LANES), F32),
        grid_spec=grid_spec,
        compiler_params=_params("arbitrary", "arbitrary"),
        name="ffn_moe",
    )(tile_expert, n_valid, xs, nw.reshape(1, D), wg, wu, wd)


def _lru_kernel(h_ref, nw_ref, win_ref, cw_ref, cb_ref, wax_ref, ba_ref, bx_ref, lam_ref, wout_ref,
                o_ref, halo_scr, carry_scr, a_scr, b_scr, hs_scr):
    tt, W = h_ref.shape[1], a_scr.shape[2]
    G = tt // SUBLANES

    @pl.when(pl.program_id(1) == 0)
    def _():
        halo_scr[...] = jnp.zeros(halo_scr.shape, F32)
        carry_scr[...] = jnp.zeros(carry_scr.shape, F32)

    x = h_ref[0]
    xn = _rmsnorm(x, nw_ref[...]).astype(BF16)
    z = _dot(xn, win_ref[...])
    gate = jax.nn.gelu(z[:, :W])
    xb = z[:, W:]
    xe = jnp.concatenate([halo_scr[...], xb], axis=0)
    halo_scr[...] = xb[tt - SUBLANES:, :]
    cw = cw_ref[...]
    xc = cb_ref[...]
    for kk in range(CONV_WIDTH):
        off = SUBLANES - (CONV_WIDTH - 1) + kk
        xc = xc + cw[kk:kk + 1, :] * xe[off:off + tt, :]
    xcb = xc.astype(BF16)
    hd = W // LRU_HEADS
    r_parts, i_parts = [], []
    for n in range(LRU_HEADS):
        ri = _dot(xcb[:, n * hd:(n + 1) * hd], wax_ref[n])
        r_parts.append(ri[:, :hd])
        i_parts.append(ri[:, hd:])
    r = jax.nn.sigmoid(jnp.concatenate(r_parts, axis=1) + ba_ref[...])
    ig = jax.nn.sigmoid(jnp.concatenate(i_parts, axis=1) + bx_ref[...])
    nl = -lam_ref[...]
    softplus = jnp.maximum(nl, 0.0) + jnp.log1p(jnp.exp(-jnp.abs(nl)))
    a = jnp.exp(-LRU_C * r * softplus)
    b = jnp.sqrt(1.0 - a * a) * (ig * xc)

    A = a.reshape(G, SUBLANES, W)
    Bv = b.reshape(G, SUBLANES, W)
    sub = lax.broadcasted_iota(I32, A.shape, 1)
    for d in (1, 2, 4):
        keep = sub >= d
        a_sh = jnp.where(keep, pltpu.roll(A, d, 1), 1.0)
        b_sh = jnp.where(keep, pltpu.roll(Bv, d, 1), 0.0)
        Bv = A * b_sh + Bv
        A = A * a_sh
    a_scr[...] = A
    b_scr[...] = Bv

    def body(g, hprev):
        hg = b_scr[g] + a_scr[g] * hprev
        hs_scr[g] = hg
        return hg[SUBLANES - 1:, :]

    carry_scr[...] = lax.fori_loop(0, G, body, carry_scr[...], unroll=8)
    y = (hs_scr[...].reshape(tt, W) * gate).astype(BF16)
    o_ref[0] = x + _dot(y, wout_ref[...])


def _lru_mixer(h, nw, w_in, conv_w, conv_b, w_a, b_a, w_x, b_x, lam, w_out):
    B, S, D = h.shape
    W = w_out.shape[0]
    tt = min(TT_LRU, S)
    hd = W // LRU_HEADS
    wax = jnp.concatenate([w_a, w_x], axis=2).astype(BF16)
    G = tt // SUBLANES
    return pl.pallas_call(
        _lru_kernel,
        out_shape=jax.ShapeDtypeStruct((B, S, D), F32),
        grid=(B, S // tt),
        in_specs=[
            pl.BlockSpec((1, tt, D), lambda b, t: (b, t, 0)),
            _const_spec((1, D)),
            _const_spec((D, 2 * W)),
            _const_spec((CONV_WIDTH, W)),
            _const_spec((1, W)),
            _const_spec((LRU_HEADS, hd, 2 * hd)),
            _const_spec((1, W)),
            _const_spec((1, W)),
            _const_spec((1, W)),
            _const_spec((W, D)),
        ],
        out_specs=pl.BlockSpec((1, tt, D), lambda b, t: (b, t, 0)),
        scratch_shapes=[
            pltpu.VMEM((SUBLANES, W), F32), pltpu.VMEM((1, W), F32),
            pltpu.VMEM((G, SUBLANES, W), F32), pltpu.VMEM((G, SUBLANES, W), F32),
            pltpu.VMEM((G, SUBLANES, W), F32),
        ],
        compiler_params=_params("parallel", "arbitrary"),
        name="rglru_mixer",
    )(h, nw.reshape(1, D), w_in.astype(BF16), conv_w, conv_b.reshape(1, W), wax, b_a.reshape(1, W),
      b_x.reshape(1, W), lam.reshape(1, W), w_out.astype(BF16))


def _pool_kernel(h_ref, halo_ref, nw_ref, wg_ref, sc_ref, o_ref, *, blocks_per_seq):
    tm, D = h_ref.shape
    gsz = D // len(POOL_WINDOWS)
    x = h_ref[...]
    first = pl.program_id(0) % blocks_per_seq == 0
    xn = _rmsnorm(x, nw_ref[...])
    hn_halo = _rmsnorm(halo_ref[...], nw_ref[...])
    hn_halo = jnp.where(first, 0.0, hn_halo)
    e = jnp.concatenate([hn_halo, xn], axis=0)
    t = (pl.program_id(0) % blocks_per_seq) * tm + lax.broadcasted_iota(I32, (tm, 1), 0)
    outs = []
    for g, w in enumerate(POOL_WINDOWS):
        cur = e[:, g * gsz:(g + 1) * gsz]
        base = 0
        d = 1
        while d < w:
            cur = cur[d:, :] + cur[:-d, :]
            base += d
            d *= 2
        win = cur[POOL_HALO - base:POOL_HALO - base + tm, :]
        count = jnp.minimum(t + 1, w).astype(F32)
        pooled = (win / count - xn[:, g * gsz:(g + 1) * gsz]).astype(BF16)
        outs.append(_dot(pooled, wg_ref[g]))
    o_ref[...] = x + jnp.concatenate(outs, axis=1) * sc_ref[...]


def _pool_mixer(h, nw, w_grp, scale, S):
    T, D = h.shape
    tm = min(TM_POOL, S)
    bps = S // tm
    hb = tm // POOL_HALO
    ng, gsz = w_grp.shape[0], w_grp.shape[1]
    return pl.pallas_call(
        functools.partial(_pool_kernel, blocks_per_seq=bps),
        out_shape=jax.ShapeDtypeStruct((T, D), F32),
        grid=(T // tm,),
        in_specs=[
            pl.BlockSpec((tm, D), lambda i: (i, 0)),
            pl.BlockSpec((POOL_HALO, D), lambda i: (jnp.maximum(i * hb - 1, 0), 0)),
            _const_spec((1, D)),
            _const_spec((ng, gsz, gsz)),
            _const_spec((1, D)),
        ],
        out_specs=pl.BlockSpec((tm, D), lambda i: (i, 0)),
        compiler_params=_params("parallel"),
        name="pool_mixer",
    )(h, h, nw.reshape(1, D), w_grp.astype(BF16), scale.reshape(1, D))


def _gmlp_kernel(h_ref, nw_ref, win_ref, lg_ref, lb_ref, ws_ref, bs_ref, wout_ref, o_ref):
    tm, D = h_ref.shape
    W = wout_ref.shape[0]
    gw = W // GMLP_GROUPS
    x = h_ref[...]
    xn = _rmsnorm(x, nw_ref[...]).astype(BF16)
    z = jax.nn.gelu(_dot(xn, win_ref[...]))
    u = z[:, :W]
    v = z[:, W:]
    mu = jnp.mean(v, axis=-1, keepdims=True)
    vc = v - mu
    var = jnp.mean(vc * vc, axis=-1, keepdims=True)
    vb = (vc * lax.rsqrt(var + LN_EPS) * lg_ref[...] + lb_ref[...]).astype(BF16)
    row = lax.broadcasted_iota(I32, (GMLP_CHUNK, GMLP_CHUNK), 0)
    col = lax.broadcasted_iota(I32, (GMLP_CHUNK, GMLP_CHUNK), 1)
    bs = bs_ref[...]
    ws = [jnp.where(col <= row, ws_ref[g], 0.0).astype(BF16) for g in range(GMLP_GROUPS)]
    rows = []
    for c in range(tm // GMLP_CHUNK):
        cols = []
        for g in range(GMLP_GROUPS):
            blk = vb[c * GMLP_CHUNK:(c + 1) * GMLP_CHUNK, g * gw:(g + 1) * gw]
            cols.append(_dot(ws[g], blk) + bs[:, g:g + 1])
        rows.append(jnp.concatenate(cols, axis=1))
    v2 = jnp.concatenate(rows, axis=0)
    o_ref[...] = x + _dot((u * v2).astype(BF16), wout_ref[...])


def _gmlp_mixer(h, nw, w_in, ln_g, ln_b, w_s, b_s, w_out):
    T, D = h.shape
    W = w_out.shape[0]
    tm = min(TM_GMLP, T)
    return pl.pallas_call(
        _gmlp_kernel,
        out_shape=jax.ShapeDtypeStruct((T, D), F32),
        grid=(T // tm,),
        in_specs=[
            pl.BlockSpec((tm, D), lambda i: (i, 0)),
            _const_spec((1, D)),
            _const_spec((D, 2 * W)),
            _const_spec((1, W)),
            _const_spec((1, W)),
            _const_spec((GMLP_GROUPS, GMLP_CHUNK, GMLP_CHUNK)),
            _const_spec((GMLP_CHUNK, GMLP_GROUPS)),
            _const_spec((W, D)),
        ],
        out_specs=pl.BlockSpec((tm, D), lambda i: (i, 0)),
        compiler_params=_params("parallel"),
        name="gmlp_mixer",
    )(h, nw.reshape(1, D), w_in.astype(BF16), ln_g.reshape(1, W), ln_b.reshape(1, W), w_s, b_s.T,
      w_out.astype(BF16))


def _router_kernel(h_ref, nw_ref, wrt_ref, meta_ref, gate_ref, cnt_ref, cnt_scr):
    tm = h_ref.shape[0]

    @pl.when(pl.program_id(0) == 0)
    def _():
        cnt_scr[...] = jnp.zeros(cnt_scr.shape, F32)

    xn = _rmsnorm(h_ref[...], nw_ref[...])
    logits = lax.dot_general(wrt_ref[...], xn, (((1,), (1,)), ((), ())),
                             precision=lax.Precision.HIGHEST, preferred_element_type=F32)
    eid = lax.broadcasted_iota(I32, logits.shape, 0).astype(F32)
    m1 = jnp.max(logits, axis=0, keepdims=True)
    e1 = jnp.min(jnp.where(logits == m1, eid, float(N_EXPERTS)), axis=0, keepdims=True)
    rest = jnp.where(eid == e1, -jnp.inf, logits)
    m2 = jnp.max(rest, axis=0, keepdims=True)
    e2 = jnp.min(jnp.where(rest == m2, eid, float(N_EXPERTS)), axis=0, keepdims=True)
    ex = jnp.exp(m2 - m1)
    g1 = 1.0 / (1.0 + ex)
    g2 = ex / (1.0 + ex)
    hit1 = eid == e1
    hit2 = eid == e2
    onehot = jnp.where(hit1 | hit2, 1.0, 0.0)
    row = lax.broadcasted_iota(I32, (tm, tm), 0)
    col = lax.broadcasted_iota(I32, (tm, tm), 1)
    earlier = jnp.where(row < col, 1.0, 0.0).astype(BF16)
    before = _dot(onehot.astype(BF16), earlier) + cnt_scr[...]
    r1 = jnp.sum(jnp.where(hit1, before, 0.0), axis=0, keepdims=True)
    r2 = jnp.sum(jnp.where(hit2, before, 0.0), axis=0, keepdims=True)
    cnt_scr[...] += jnp.sum(onehot, axis=1, keepdims=True)
    zero = jnp.zeros((SUBLANES - 4, tm), F32)
    meta_ref[...] = jnp.concatenate([e1, e2, r1, r2, zero], axis=0).astype(I32)
    gate_ref[...] = jnp.concatenate([g1, g2, zero, 0.0 * g1, 0.0 * g2], axis=0)
    cnt_ref[...] = jnp.broadcast_to(cnt_scr[...], cnt_ref.shape)


def _router(h, nw, w_router):
    T, D = h.shape
    assert N_EXPERTS == SUBLANES
    tm = min(TM_ROUTE, T)
    tok_spec = pl.BlockSpec((SUBLANES, tm), lambda i: (0, i))
    return pl.pallas_call(
        _router_kernel,
        out_shape=(jax.ShapeDtypeStruct((SUBLANES, T), I32), jax.ShapeDtypeStruct((SUBLANES, T), F32),
                   jax.ShapeDtypeStruct((N_EXPERTS, LANES), F32)),
        grid=(T // tm,),
        in_specs=[pl.BlockSpec((tm, D), lambda i: (i, 0)), _const_spec((1, D)), _const_spec((N_EXPERTS, D))],
        out_specs=(tok_spec, tok_spec, _const_spec((N_EXPERTS, LANES))),
        scratch_shapes=[pltpu.VMEM((N_EXPERTS, 1), F32)],
        compiler_params=_params("arbitrary"),
        name="moe_router",
    )(h, nw.reshape(1, D), w_router.T)


ZERO_ROWS = 512


def _dispatch_kernel(zs_ref, zl_ref, d1_ref, d2_ref, h_ref, xs_ref, rows_scr, zero_scr, sem, zsem):
    tm = h_ref.shape[0]
    _tile_to_rows(rows_scr, h_ref[...], tm)

    def copies(r):
        src = rows_scr.at[pl.ds(pl.multiple_of(r * SUBLANES, SUBLANES), SUBLANES), :]
        return [pltpu.make_async_copy(
            src, xs_ref.at[pl.ds(pl.multiple_of(d[0, 0, r] * SUBLANES, SUBLANES), SUBLANES), :], sem)
            for d in (d1_ref, d2_ref)]

    def start(r, c):
        for prio, cp in enumerate(copies(r)):
            cp.start(priority=prio)
        return c

    def wait(r, c):
        for cp in copies(r):
            cp.wait()
        return c

    def zero_copies(fn):
        def zero_dma(off, size, pred):
            cp = pltpu.make_async_copy(
                zero_scr.at[pl.ds(0, size * SUBLANES), :],
                xs_ref.at[pl.ds(pl.multiple_of(off * SUBLANES, SUBLANES), size * SUBLANES), :], zsem)
            pl.when(pred)(functools.partial(fn, cp))

        for k in range(N_EXPERTS + 1):
            tail = k == N_EXPERTS
            max_rows = 2 * ZERO_ROWS * (N_EXPERTS if tail else 1)
            n = zl_ref[k]
            whole = n // ZERO_ROWS
            for c in range(max_rows // ZERO_ROWS):
                zero_dma(zs_ref[k] + c * ZERO_ROWS, ZERO_ROWS, c < whole)
            base = zs_ref[k] + whole * ZERO_ROWS
            rest = n % ZERO_ROWS
            size = ZERO_ROWS // 2
            while size >= 1:
                zero_dma(base + (rest // (2 * size)) * (2 * size), size, (rest // size) % 2 == 1)
                size //= 2

    first = pl.program_id(0) == 0

    @pl.when(first)
    def _():
        zero_scr[...] = jnp.zeros(zero_scr.shape, F32)
        zero_copies(lambda cp: cp.start())

    lax.fori_loop(0, tm, start, 0, unroll=4)
    lax.fori_loop(0, tm, wait, 0, unroll=4)

    @pl.when(first)
    def _():
        zero_copies(lambda cp: cp.wait())


def _dispatch(h, d1, d2, zero_start, zero_len, P):
    T, D = h.shape
    tm = min(TM_DISPATCH, T)
    nb = T // tm
    smem = functools.partial(pl.BlockSpec, memory_space=pltpu.SMEM)
    grid_spec = pltpu.PrefetchScalarGridSpec(
        num_scalar_prefetch=2,
        grid=(nb,),
        in_specs=[
            smem((1, 1, tm), lambda i, zs, zl: (i, 0, 0)),
            smem((1, 1, tm), lambda i, zs, zl: (i, 0, 0)),
            pl.BlockSpec((tm, D), lambda i, zs, zl: (i, 0)),
        ],
        out_specs=pl.BlockSpec(memory_space=pl.ANY),
        scratch_shapes=[pltpu.VMEM((tm * SUBLANES, LANES), F32), pltpu.VMEM((ZERO_ROWS * SUBLANES, LANES), F32),
                        pltpu.SemaphoreType.DMA, pltpu.SemaphoreType.DMA],
    )
    return pl.pallas_call(
        _dispatch_kernel,
        out_shape=jax.ShapeDtypeStruct((P * SUBLANES, LANES), F32),
        grid_spec=grid_spec,
        compiler_params=_params("arbitrary"),
        name="moe_dispatch",
    )(zero_start, zero_len, d1.reshape(nb, 1, tm), d2.reshape(nb, 1, tm), h)


def _combine_kernel(d1_ref, d2_ref, h_ref, gate_ref, ys_ref, *rest, final_norm):
    if final_norm:
        fw_ref, o_ref, buf1, buf2, sem = rest
    else:
        o_ref, buf1, buf2, sem = rest
    tm = h_ref.shape[0]

    def copies(r):
        dst = pl.ds(pl.multiple_of(r * SUBLANES, SUBLANES), SUBLANES)
        return [pltpu.make_async_copy(
            ys_ref.at[pl.ds(pl.multiple_of(d[0, 0, r] * SUBLANES, SUBLANES), SUBLANES), :], buf.at[dst, :], sem)
            for d, buf in ((d1_ref, buf1), (d2_ref, buf2))]

    def start(r, c):
        for prio, cp in enumerate(copies(r)):
            cp.start(priority=prio)
        return c

    def wait(r, c):
        for cp in copies(r):
            cp.wait()
        return c

    lax.fori_loop(0, tm, start, 0, unroll=4)
    lax.fori_loop(0, tm, wait, 0, unroll=4)
    gates = gate_ref[...]
    out = h_ref[...] + gates[:, 0:1] * _rows_to_tile(buf1, tm) + gates[:, 1:2] * _rows_to_tile(buf2, tm)
    if final_norm:
        out = _rmsnorm(out, fw_ref[...])
    o_ref[...] = out


def _combine(h, gates, ys, d1, d2, final_w=None):
    T, D = h.shape
    tm = min(TM_COMBINE, T)
    nb = T // tm
    smem = functools.partial(pl.BlockSpec, memory_space=pltpu.SMEM)
    in_specs = [
        smem((1, 1, tm), lambda i: (i, 0, 0)),
        smem((1, 1, tm), lambda i: (i, 0, 0)),
        pl.BlockSpec((tm, D), lambda i: (i, 0)),
        pl.BlockSpec((tm, TOP_K), lambda i: (i, 0)),
        pl.BlockSpec(memory_space=pl.ANY),
    ]
    args = [d1.reshape(nb, 1, tm), d2.reshape(nb, 1, tm), h, gates, ys]
    if final_w is not None:
        in_specs.append(_const_spec((1, D)))
        args.append(final_w.reshape(1, D))
    buf = pltpu.VMEM((tm * SUBLANES, LANES), F32)
    return pl.pallas_call(
        functools.partial(_combine_kernel, final_norm=final_w is not None),
        out_shape=jax.ShapeDtypeStruct((T, D), F32),
        grid=(nb,),
        in_specs=in_specs,
        out_specs=pl.BlockSpec((tm, D), lambda i: (i, 0)),
        scratch_shapes=[buf, buf, pltpu.SemaphoreType.DMA],
        compiler_params=_params("arbitrary"),
        name="moe_combine",
    )(*args)


def _moe_layer(h, nw, w_router, wg, wu, wd, layer, final_w=None):
    T, D = h.shape
    meta, gates, counts = _router(h, nw, w_router)
    tm = min(TM_MOE, T)
    sizes = counts[:, 0].astype(I32)
    padded = (sizes + tm - 1) // tm * tm
    pad_end = jnp.cumsum(padded)
    pad_start = pad_end - padded
    d1 = pad_start[meta[0]] + meta[2]
    d2 = pad_start[meta[1]] + meta[3]
    n_tiles = (T * TOP_K + N_EXPERTS * (tm - 1)) // tm
    tile_start = jnp.arange(n_tiles, dtype=I32) * tm
    tile_expert = jnp.minimum(jnp.sum(pad_end[None, :] <= tile_start[:, None], axis=1), N_EXPERTS - 1)
    n_valid = (pad_end[-1] // tm).reshape(1)
    assert tm <= 2 * ZERO_ROWS
    total = n_tiles * tm
    zero_start = jnp.concatenate([pad_start + sizes, pad_end[-1:]])
    zero_len = jnp.concatenate([padded - sizes, total - pad_end[-1:]])
    xs = _dispatch(h, d1, d2, zero_start, zero_len, total)
    ys = _ffn_moe(xs, nw, wg, wu, wd, layer, tile_expert.astype(I32), n_valid.astype(I32), tm)
    return _combine(h, gates[:TOP_K].T, ys, d1, d2, final_w)


def kernel(x, positions, norm_mix, norm_ffn, norm_final, mla_w_in, mla_q_norm, mla_kv_norm, mla_w_uq,
           mla_w_ukv, mla_w_o, lru_w_in, lru_conv_w, lru_conv_b, lru_w_a, lru_b_a, lru_w_x, lru_b_x,
           lru_lam, lru_w_out, pool_w_grp, pool_scale, gmlp_w_in, gmlp_ln_g, gmlp_ln_b, gmlp_w_s,
           gmlp_b_s, gmlp_w_out, ffn_w_gate, ffn_w_up, ffn_w_down, moe_w_router, moe_w_gate, moe_w_up,
           moe_w_down):
    B, S, D = x.shape
    T = B * S
    depth = norm_mix.shape[0]
    h = x.reshape(T, D)
    for i in range(depth):
        m, j = i % 4, i // 4
        if m == 0:
            q, k, v = _mla_proj(h, positions.reshape(1, T), norm_mix[i], mla_w_in[j], mla_q_norm[j],
                                mla_kv_norm[j], mla_w_uq[j], mla_w_ukv[j], B, S)
            o = _attention(q, k, v)
            h = _matmul_residual(o.reshape(T, -1), mla_w_o[j], h)
        elif m == 1:
            h = _lru_mixer(h.reshape(B, S, D), norm_mix[i], lru_w_in[j], lru_conv_w[j], lru_conv_b[j],
                           lru_w_a[j], lru_b_a[j], lru_w_x[j], lru_b_x[j], lru_lam[j],
                           lru_w_out[j]).reshape(T, D)
        elif m == 2:
            h = _pool_mixer(h, norm_mix[i], pool_w_grp[j], pool_scale[j], S)
        else:
            h = _gmlp_mixer(h, norm_mix[i], gmlp_w_in[j], gmlp_ln_g[j], gmlp_ln_b[j], gmlp_w_s[j],
                            gmlp_b_s[j], gmlp_w_out[j])
        kk = i // 2
        if i % 2 == 0:
            h = _ffn_dense(h, norm_ffn[i], ffn_w_gate, ffn_w_up, ffn_w_down, kk)
        else:
            last = i == depth - 1
            h = _moe_layer(h, norm_ffn[i], moe_w_router[kk], moe_w_gate, moe_w_up, moe_w_down, kk,
                           norm_final if last else None)
    if depth % 2 == 1:
        raise NotImplementedError("final RMSNorm is fused into the last MoE combine")
    return h.reshape(B, S, D)
```

```python
import functools

import jax
import jax.numpy as jnp
from jax import lax
from jax.experimental import pallas as pl
from jax.experimental.pallas import tpu as pltpu

F32 = jnp.float32
BF16 = jnp.bfloat16
I32 = jnp.int32

RMS_EPS = 1e-6
LN_EPS = 1e-5

MLA_HEADS = 16
MLA_Q_RANK = 384
MLA_KV_RANK = 256
MLA_NOPE = 64
MLA_ROPE = 32
MLA_V = 64
ROPE_BASE = 10000.0
LRU_HEADS = 8
CONV_WIDTH = 4
LRU_C = 8.0
POOL_WINDOWS = (2, 4, 8, 16)
GMLP_GROUPS = 4
GMLP_CHUNK = 128
N_EXPERTS = 8
TOP_K = 2

LANES = 128
SUBLANES = 8
VMEM_LIMIT_BYTES = 56 * 1024 * 1024

TM_PROJ = 512
TQ_ATTN = 512
HEADS_PER_STEP = 4
SUM_ROWS = 16
TM_RES = 512
TM_FFN = 1024
TF_FFN = 512
TT_LRU = 512
TM_POOL = 512
TM_GMLP = 512
TM_ROUTE = 512
TM_MOE = 1024
TM_DISPATCH = 512
TM_COMBINE = 256
POOL_HALO = 16


def _params(*sem):
    return pltpu.CompilerParams(dimension_semantics=sem, vmem_limit_bytes=VMEM_LIMIT_BYTES)


def _rmsnorm(x, g):
    return x * lax.rsqrt(jnp.mean(x * x, axis=-1, keepdims=True) + RMS_EPS) * g


def _dot(a, b):
    return jnp.dot(a, b, preferred_element_type=F32)


def _const_spec(shape):
    return pl.BlockSpec(shape, lambda *_: (0,) * len(shape))


def _mla_proj_kernel(h_ref, pos_ref, nw_ref, win_ref, qn_ref, kvn_ref, wuqt_ref, wuk_ref, wuvt_ref,
                     invf_ref, q_ref, k_ref, v_ref, *, scale):
    half = MLA_ROPE // 2
    x = h_ref[...]
    tm = x.shape[0]
    xn = _rmsnorm(x, nw_ref[...]).astype(BF16)
    lat = _dot(xn, win_ref[...])
    cq = _rmsnorm(lat[:, :MLA_Q_RANK], qn_ref[...]).astype(BF16)
    ckv = _rmsnorm(lat[:, MLA_Q_RANK:MLA_Q_RANK + MLA_KV_RANK], kvn_ref[...]).astype(BF16)
    kr = lat[:, MLA_Q_RANK + MLA_KV_RANK:]
    ang = invf_ref[...] * pos_ref[...].astype(F32)
    c = jnp.cos(ang)
    s = jnp.sin(ang)
    pad_r = LANES - MLA_NOPE - MLA_ROPE
    one = jnp.ones((MLA_NOPE, tm), F32)
    c_t = jnp.concatenate([one, c, c, jnp.ones((pad_r, tm), F32)], axis=0).T
    s_lo = jnp.concatenate([0.0 * one, -s, jnp.zeros((half + pad_r, tm), F32)], axis=0).T
    s_hi = jnp.concatenate([0.0 * one, 0.0 * s, s, jnp.zeros((pad_r, tm), F32)], axis=0).T
    kr = kr * c_t + pltpu.roll(kr, LANES - half, 1) * s_lo + pltpu.roll(kr, half, 1) * s_hi
    kn = _dot(ckv, wuk_ref[...])
    for hh in range(MLA_HEADS):
        k_ref[0, hh] = (kn[:, hh * LANES:(hh + 1) * LANES] + kr).astype(BF16)
    nt = (((1,), (1,)), ((), ()))
    qt = lax.dot_general(wuqt_ref[...], cq, nt, preferred_element_type=F32)
    for hh in range(MLA_HEADS):
        blk = qt[hh * LANES:(hh + 1) * LANES]
        x1 = blk[MLA_NOPE:MLA_NOPE + half]
        x2 = blk[MLA_NOPE + half:MLA_NOPE + MLA_ROPE]
        roped = jnp.concatenate(
            [blk[:MLA_NOPE], x1 * c - x2 * s, x2 * c + x1 * s, blk[MLA_NOPE + MLA_ROPE:]], axis=0)
        q_ref[0, hh] = (roped * scale).astype(BF16)
    v_ref[0, 0] = lax.dot_general(wuvt_ref[...], ckv, nt, preferred_element_type=F32).astype(BF16)


def _mla_proj(h, pos, nw, w_in, q_norm, kv_norm, w_uq, w_ukv, B, S):
    T, D = h.shape
    H = MLA_HEADS
    tm = min(TM_PROJ, S)
    ns = S // tm
    qk = MLA_NOPE + MLA_ROPE
    pad_r = LANES - qk
    w_in_p = jnp.concatenate([
        w_in[:, :MLA_Q_RANK + MLA_KV_RANK],
        jnp.zeros((D, MLA_NOPE), F32), w_in[:, MLA_Q_RANK + MLA_KV_RANK:], jnp.zeros((D, pad_r), F32)],
        axis=1).astype(BF16)
    wq = w_uq.reshape(MLA_Q_RANK, H, qk)
    wq_t = jnp.pad(wq, ((0, 0), (0, 0), (0, pad_r))).reshape(MLA_Q_RANK, H * LANES).T.astype(BF16)
    wkv = w_ukv.reshape(MLA_KV_RANK, H, MLA_NOPE + MLA_V)
    wk_p = jnp.pad(wkv[:, :, :MLA_NOPE], ((0, 0), (0, 0), (0, LANES - MLA_NOPE)))
    wk_p = wk_p.reshape(MLA_KV_RANK, H * LANES).astype(BF16)
    wv_t = wkv[:, :, MLA_NOPE:].reshape(MLA_KV_RANK, H * MLA_V).T.astype(BF16)
    half = MLA_ROPE // 2
    invf = (ROPE_BASE ** (-jnp.arange(half, dtype=F32) / half)).reshape(half, 1)
    wl = w_in_p.shape[1]
    out_shape = (
        jax.ShapeDtypeStruct((B, H, LANES, S), BF16),
        jax.ShapeDtypeStruct((B, H, S, LANES), BF16),
        jax.ShapeDtypeStruct((B, ns, H * MLA_V, tm), BF16),
    )
    scale = qk ** -0.5 * 1.4426950408889634
    return pl.pallas_call(
        functools.partial(_mla_proj_kernel, scale=scale),
        out_shape=out_shape,
        grid=(T // tm,),
        in_specs=[
            pl.BlockSpec((tm, D), lambda i: (i, 0)),
            pl.BlockSpec((1, tm), lambda i: (0, i)),
            _const_spec((1, D)),
            _const_spec((D, wl)),
            _const_spec((1, MLA_Q_RANK)),
            _const_spec((1, MLA_KV_RANK)),
            _const_spec((H * LANES, MLA_Q_RANK)),
            _const_spec((MLA_KV_RANK, H * LANES)),
            _const_spec((H * MLA_V, MLA_KV_RANK)),
            _const_spec((half, 1)),
        ],
        out_specs=(pl.BlockSpec((1, H, LANES, tm), lambda i: (i // ns, 0, 0, i % ns)),
                   pl.BlockSpec((1, H, tm, LANES), lambda i: (i // ns, 0, i % ns, 0)),
                   pl.BlockSpec((1, 1, H * MLA_V, tm), lambda i: (i // ns, i % ns, 0, 0))),
        compiler_params=_params("parallel"),
        name="mla_proj",
    )(h, pos, nw.reshape(1, D), w_in_p, q_norm.reshape(1, -1), kv_norm.reshape(1, -1), wq_t, wk_p, wv_t,
      invf)


def _attn_kernel(q_ref, k_ref, v_ref, o_ref, m_scr, acc_scr, s_scr, *, tq):
    qi = pl.program_id(2)
    nh = q_ref.shape[1]
    m_scr[...] = jnp.full(m_scr.shape, -jnp.inf, F32)
    acc_scr[...] = jnp.zeros(acc_scr.shape, F32)

    def scores(j, hh):
        return _dot(k_ref[0, hh, pl.ds(pl.multiple_of(j * tq, tq), tq), :], q_ref[0, hh])

    ones = jnp.ones((SUM_ROWS, tq), BF16)

    def softmax_pv(s, j, hh, masked):
        if masked:
            kv = lax.broadcasted_iota(I32, s.shape, 0)
            qq = lax.broadcasted_iota(I32, s.shape, 1)
            s = jnp.where(kv <= qq, s, -jnp.inf)
        m_prev = m_scr[hh]
        m_new = jnp.maximum(m_prev, jnp.max(s, axis=0, keepdims=True))
        alpha = jnp.exp2(m_prev - m_new)
        p = jnp.exp2(s - m_new).astype(BF16)
        vt = jnp.concatenate([v_ref[0, j, hh * MLA_V:(hh + 1) * MLA_V, :], ones], axis=0)
        acc_scr[hh] = alpha * acc_scr[hh] + _dot(vt, p)
        m_scr[hh] = m_new

    def stage(j, slot, masked, prefetch):
        for hh in range(nh):
            s = s_scr[slot, hh]
            if prefetch:
                s_scr[1 - slot, hh] = scores(j + 1, hh)
            softmax_pv(s, j, hh, masked)

    for hh in range(nh):
        s_scr[0, hh] = scores(0, hh)

    def pair(i, carry):
        stage(2 * i, 0, False, True)
        stage(2 * i + 1, 1, False, True)
        return carry

    lax.fori_loop(0, qi // 2, pair, 0)

    @pl.when(qi % 2 == 1)
    def _():
        stage(qi - 1, 0, False, True)
        stage(qi, 1, True, False)

    @pl.when(qi % 2 == 0)
    def _():
        stage(qi, 0, True, False)
    out = jnp.concatenate([acc_scr[hh, :MLA_V] / acc_scr[hh, MLA_V:MLA_V + 1] for hh in range(nh)], axis=0)
    o_ref[0] = out.T.astype(BF16)


def _attention(qt, k, vt):
    B, H, S, _ = k.shape
    tq = vt.shape[3]
    nh = HEADS_PER_STEP
    return pl.pallas_call(
        functools.partial(_attn_kernel, tq=tq),
        out_shape=jax.ShapeDtypeStruct((B, S, H * MLA_V), BF16),
        grid=(B, H // nh, S // tq),
        in_specs=[
            pl.BlockSpec((1, nh, LANES, tq), lambda b, hp, qi: (b, hp, 0, qi)),
            pl.BlockSpec((1, nh, S, LANES), lambda b, hp, qi: (b, hp, 0, 0)),
            pl.BlockSpec((1, S // tq, nh * MLA_V, tq), lambda b, hp, qi: (b, 0, hp, 0)),
        ],
        out_specs=pl.BlockSpec((1, tq, nh * MLA_V), lambda b, hp, qi: (b, qi, hp)),
        scratch_shapes=[pltpu.VMEM((nh, 1, tq), F32), pltpu.VMEM((nh, MLA_V + SUM_ROWS, tq), F32),
                        pltpu.VMEM((2, nh, tq, tq), F32)],
        compiler_params=_params("parallel", "parallel", "arbitrary"),
        name="mla_attention",
    )(qt, k, vt)


def _matmul_res_kernel(x_ref, w_ref, h_ref, o_ref):
    o_ref[...] = h_ref[...] + _dot(x_ref[...], w_ref[...])


def _matmul_residual(x, w, h):
    T, K = x.shape
    N = w.shape[1]
    tm = min(TM_RES, T)
    return pl.pallas_call(
        _matmul_res_kernel,
        out_shape=jax.ShapeDtypeStruct((T, N), F32),
        grid=(T // tm,),
        in_specs=[pl.BlockSpec((tm, K), lambda i: (i, 0)), _const_spec((K, N)),
                  pl.BlockSpec((tm, N), lambda i: (i, 0))],
        out_specs=pl.BlockSpec((tm, N), lambda i: (i, 0)),
        compiler_params=_params("parallel"),
        name="matmul_residual",
    )(x, w.astype(BF16), h)


def _swiglu_steps(load_x, nw_ref, wg_ref, wu_ref, wd_ref, xn_scr, acc_scr):
    j = pl.program_id(1)

    @pl.when(j == 0)
    def _():
        xn_scr[...] = _rmsnorm(load_x(), nw_ref[...]).astype(BF16)
        acc_scr[...] = jnp.zeros(acc_scr.shape, F32)

    xn = xn_scr[...]
    g = _dot(xn, wg_ref[...].astype(BF16))
    u = _dot(xn, wu_ref[...].astype(BF16))
    hmid = (g * jax.nn.sigmoid(g) * u).astype(BF16)
    acc_scr[...] += _dot(hmid, wd_ref[...].astype(BF16))


def _ffn_dense_kernel(x_ref, nw_ref, wg_ref, wu_ref, wd_ref, o_ref, xn_scr, acc_scr):
    _swiglu_steps(lambda: x_ref[...], nw_ref, wg_ref, wu_ref, wd_ref, xn_scr, acc_scr)

    @pl.when(pl.program_id(1) == pl.num_programs(1) - 1)
    def _():
        o_ref[...] = x_ref[...] + acc_scr[...]


def _ffn_dense(h, nw, wg, wu, wd, layer):
    T, D = h.shape
    FF = wg.shape[2]
    tm = min(TM_FFN, T)
    tf = min(TF_FFN, FF)
    return pl.pallas_call(
        _ffn_dense_kernel,
        out_shape=jax.ShapeDtypeStruct((T, D), F32),
        grid=(T // tm, FF // tf),
        in_specs=[
            pl.BlockSpec((tm, D), lambda i, j: (i, 0)),
            _const_spec((1, D)),
            pl.BlockSpec((None, D, tf), lambda i, j: (layer, 0, j)),
            pl.BlockSpec((None, D, tf), lambda i, j: (layer, 0, j)),
            pl.BlockSpec((None, tf, D), lambda i, j: (layer, j, 0)),
        ],
        out_specs=pl.BlockSpec((tm, D), lambda i, j: (i, 0)),
        scratch_shapes=[pltpu.VMEM((tm, D), BF16), pltpu.VMEM((tm, D), F32)],
        compiler_params=_params("parallel", "arbitrary"),
        name="ffn_dense",
    )(h, nw.reshape(1, D), wg, wu, wd)


def _rows_to_tile(ref, tm):
    return jnp.concatenate([ref[pl.ds(s, tm, stride=SUBLANES), :] for s in range(SUBLANES)], axis=1)


def _tile_to_rows(ref, x, tm):
    for s in range(SUBLANES):
        ref[pl.ds(s, tm, stride=SUBLANES), :] = x[:, s * LANES:(s + 1) * LANES]


def _ffn_moe_kernel(te_ref, nv_ref, x_ref, nw_ref, wg_ref, wu_ref, wd_ref, o_ref, xn_scr, acc_scr):
    del te_ref
    tm = xn_scr.shape[0]
    valid = pl.program_id(0) < nv_ref[0]
    last = pl.program_id(1) == pl.num_programs(1) - 1

    @pl.when(valid)
    def _():
        _swiglu_steps(lambda: _rows_to_tile(x_ref, tm), nw_ref, wg_ref, wu_ref, wd_ref, xn_scr, acc_scr)

        @pl.when(last)
        def _():
            _tile_to_rows(o_ref, acc_scr[...], tm)

    @pl.when(jnp.logical_and(jnp.logical_not(valid), last))
    def _():
        o_ref[...] = jnp.zeros(o_ref.shape, F32)


def _ffn_moe(xs, nw, wg, wu, wd, layer, tile_expert, n_valid, tm):
    D = wg.shape[2]
    FF = wg.shape[3]
    P = xs.shape[0] // SUBLANES
    tf = min(TF_FFN, FF)
    nj = FF // tf

    def tile(i, nv):
        return jnp.maximum(jnp.minimum(i, nv[0] - 1), 0)

    def col(i, j, nv):
        return jnp.where(i < nv[0], j, nj - 1)

    def up_map(i, j, te, nv):
        return (layer, te[tile(i, nv)], 0, col(i, j, nv))

    def down_map(i, j, te, nv):
        return (layer, te[tile(i, nv)], col(i, j, nv), 0)

    grid_spec = pltpu.PrefetchScalarGridSpec(
        num_scalar_prefetch=2,
        grid=(P // tm, nj),
        in_specs=[
            pl.BlockSpec((tm * SUBLANES, LANES), lambda i, j, te, nv: (tile(i, nv), 0)),
            pl.BlockSpec((1, D), lambda i, j, te, nv: (0, 0)),
            pl.BlockSpec((None, None, D, tf), up_map),
            pl.BlockSpec((None, None, D, tf), up_map),
            pl.BlockSpec((None, None, tf, D), down_map),
        ],
        out_specs=pl.BlockSpec((tm * SUBLANES, LANES), lambda i, j, te, nv: (i, 0)),
        scratch_shapes=[pltpu.VMEM((tm, D), BF16), pltpu.VMEM((tm, D), F32)],
    )
    return pl.pallas_call(
        _ffn_moe_kernel,
        out_shape=jax.ShapeDtypeStruct((P * SUBLANES, LANES), F32),
        grid_spec=grid_spec,
        compiler_params=_params("arbitrary", "arbitrary"),
        name="ffn_moe",
    )(tile_expert, n_valid, xs, nw.reshape(1, D), wg, wu, wd)


def _lru_kernel(h_ref, nw_ref, win_ref, cw_ref, cb_ref, wax_ref, ba_ref, bx_ref, lam_ref, wout_ref,
                o_ref, halo_scr, carry_scr, a_scr, b_scr, hs_scr):
    tt, W = h_ref.shape[1], a_scr.shape[2]
    G = tt // SUBLANES

    @pl.when(pl.program_id(1) == 0)
    def _():
        halo_scr[...] = jnp.zeros(halo_scr.shape, F32)
        carry_scr[...] = jnp.zeros(carry_scr.shape, F32)

    x = h_ref[0]
    xn = _rmsnorm(x, nw_ref[...]).astype(BF16)
    z = _dot(xn, win_ref[...])
    gate = jax.nn.gelu(z[:, :W])
    xb = z[:, W:]
    xe = jnp.concatenate([halo_scr[...], xb], axis=0)
    halo_scr[...] = xb[tt - SUBLANES:, :]
    cw = cw_ref[...]
    xc = cb_ref[...]
    for kk in range(CONV_WIDTH):
        off = SUBLANES - (CONV_WIDTH - 1) + kk
        xc = xc + cw[kk:kk + 1, :] * xe[off:off + tt, :]
    xcb = xc.astype(BF16)
    hd = W // LRU_HEADS
    r_parts, i_parts = [], []
    for n in range(LRU_HEADS):
        ri = _dot(xcb[:, n * hd:(n + 1) * hd], wax_ref[n])
        r_parts.append(ri[:, :hd])
        i_parts.append(ri[:, hd:])
    r = jax.nn.sigmoid(jnp.concatenate(r_parts, axis=1) + ba_ref[...])
    ig = jax.nn.sigmoid(jnp.concatenate(i_parts, axis=1) + bx_ref[...])
    nl = -lam_ref[...]
    softplus = jnp.maximum(nl, 0.0) + jnp.log1p(jnp.exp(-jnp.abs(nl)))
    a = jnp.exp(-LRU_C * r * softplus)
    b = jnp.sqrt(1.0 - a * a) * (ig * xc)

    A = a.reshape(G, SUBLANES, W)
    Bv = b.reshape(G, SUBLANES, W)
    sub = lax.broadcasted_iota(I32, A.shape, 1)
    for d in (1, 2, 4):
        keep = sub >= d
        a_sh = jnp.where(keep, pltpu.roll(A, d, 1), 1.0)
        b_sh = jnp.where(keep, pltpu.roll(Bv, d, 1), 0.0)
        Bv = A * b_sh + Bv
        A = A * a_sh
    a_scr[...] = A
    b_scr[...] = Bv

    def body(g, hprev):
        hg = b_scr[g] + a_scr[g] * hprev
        hs_scr[g] = hg
        return hg[SUBLANES - 1:, :]

    carry_scr[...] = lax.fori_loop(0, G, body, carry_scr[...], unroll=8)
    y = (hs_scr[...].reshape(tt, W) * gate).astype(BF16)
    o_ref[0] = x + _dot(y, wout_ref[...])


def _lru_mixer(h, nw, w_in, conv_w, conv_b, w_a, b_a, w_x, b_x, lam, w_out):
    B, S, D = h.shape
    W = w_out.shape[0]
    tt = min(TT_LRU, S)
    hd = W // LRU_HEADS
    wax = jnp.concatenate([w_a, w_x], axis=2).astype(BF16)
    G = tt // SUBLANES
    return pl.pallas_call(
        _lru_kernel,
        out_shape=jax.ShapeDtypeStruct((B, S, D), F32),
        grid=(B, S // tt),
        in_specs=[
            pl.BlockSpec((1, tt, D), lambda b, t: (b, t, 0)),
            _const_spec((1, D)),
            _const_spec((D, 2 * W)),
            _const_spec((CONV_WIDTH, W)),
            _const_spec((1, W)),
            _const_spec((LRU_HEADS, hd, 2 * hd)),
            _const_spec((1, W)),
            _const_spec((1, W)),
            _const_spec((1, W)),
            _const_spec((W, D)),
        ],
        out_specs=pl.BlockSpec((1, tt, D), lambda b, t: (b, t, 0)),
        scratch_shapes=[
            pltpu.VMEM((SUBLANES, W), F32), pltpu.VMEM((1, W), F32),
            pltpu.VMEM((G, SUBLANES, W), F32), pltpu.VMEM((G, SUBLANES, W), F32),
            pltpu.VMEM((G, SUBLANES, W), F32),
        ],
        compiler_params=_params("parallel", "arbitrary"),
        name="rglru_mixer",
    )(h, nw.reshape(1, D), w_in.astype(BF16), conv_w, conv_b.reshape(1, W), wax, b_a.reshape(1, W),
      b_x.reshape(1, W), lam.reshape(1, W), w_out.astype(BF16))


def _pool_kernel(h_ref, halo_ref, nw_ref, wg_ref, sc_ref, o_ref, *, blocks_per_seq):
    tm, D = h_ref.shape
    gsz = D // len(POOL_WINDOWS)
    x = h_ref[...]
    first = pl.program_id(0) % blocks_per_seq == 0
    xn = _rmsnorm(x, nw_ref[...])
    hn_halo = _rmsnorm(halo_ref[...], nw_ref[...])
    hn_halo = jnp.where(first, 0.0, hn_halo)
    e = jnp.concatenate([hn_halo, xn], axis=0)
    t = (pl.program_id(0) % blocks_per_seq) * tm + lax.broadcasted_iota(I32, (tm, 1), 0)
    outs = []
    for g, w in enumerate(POOL_WINDOWS):
        cur = e[:, g * gsz:(g + 1) * gsz]
        base = 0
        d = 1
        while d < w:
            cur = cur[d:, :] + cur[:-d, :]
            base += d
            d *= 2
        win = cur[POOL_HALO - base:POOL_HALO - base + tm, :]
        count = jnp.minimum(t + 1, w).astype(F32)
        pooled = (win / count - xn[:, g * gsz:(g + 1) * gsz]).astype(BF16)
        outs.append(_dot(pooled, wg_ref[g]))
    o_ref[...] = x + jnp.concatenate(outs, axis=1) * sc_ref[...]


def _pool_mixer(h, nw, w_grp, scale, S):
    T, D = h.shape
    tm = min(TM_POOL, S)
    bps = S // tm
    hb = tm // POOL_HALO
    ng, gsz = w_grp.shape[0], w_grp.shape[1]
    return pl.pallas_call(
        functools.partial(_pool_kernel, blocks_per_seq=bps),
        out_shape=jax.ShapeDtypeStruct((T, D), F32),
        grid=(T // tm,),
        in_specs=[
            pl.BlockSpec((tm, D), lambda i: (i, 0)),
            pl.BlockSpec((POOL_HALO, D), lambda i: (jnp.maximum(i * hb - 1, 0), 0)),
            _const_spec((1, D)),
            _const_spec((ng, gsz, gsz)),
            _const_spec((1, D)),
        ],
        out_specs=pl.BlockSpec((tm, D), lambda i: (i, 0)),
        compiler_params=_params("parallel"),
        name="pool_mixer",
    )(h, h, nw.reshape(1, D), w_grp.astype(BF16), scale.reshape(1, D))


def _gmlp_kernel(h_ref, nw_ref, win_ref, lg_ref, lb_ref, ws_ref, bs_ref, wout_ref, o_ref):
    tm, D = h_ref.shape
    W = wout_ref.shape[0]
    gw = W // GMLP_GROUPS
    x = h_ref[...]
    xn = _rmsnorm(x, nw_ref[...]).astype(BF16)
    z = jax.nn.gelu(_dot(xn, win_ref[...]))
    u = z[:, :W]
    v = z[:, W:]
    mu = jnp.mean(v, axis=-1, keepdims=True)
    vc = v - mu
    var = jnp.mean(vc * vc, axis=-1, keepdims=True)
    vb = (vc * lax.rsqrt(var + LN_EPS) * lg_ref[...] + lb_ref[...]).astype(BF16)
    row = lax.broadcasted_iota(I32, (GMLP_CHUNK, GMLP_CHUNK), 0)
    col = lax.broadcasted_iota(I32, (GMLP_CHUNK, GMLP_CHUNK), 1)
    bs = bs_ref[...]
    ws = [jnp.where(col <= row, ws_ref[g], 0.0).astype(BF16) for g in range(GMLP_GROUPS)]
    rows = []
    for c in range(tm // GMLP_CHUNK):
        cols = []
        for g in range(GMLP_GROUPS):
            blk = vb[c * GMLP_CHUNK:(c + 1) * GMLP_CHUNK, g * gw:(g + 1) * gw]
            cols.append(_dot(ws[g], blk) + bs[:, g:g + 1])
        rows.append(jnp.concatenate(cols, axis=1))
    v2 = jnp.concatenate(rows, axis=0)
    o_ref[...] = x + _dot((u * v2).astype(BF16), wout_ref[...])


def _gmlp_mixer(h, nw, w_in, ln_g, ln_b, w_s, b_s, w_out):
    T, D = h.shape
    W = w_out.shape[0]
    tm = min(TM_GMLP, T)
    return pl.pallas_call(
        _gmlp_kernel,
        out_shape=jax.ShapeDtypeStruct((T, D), F32),
        grid=(T // tm,),
        in_specs=[
            pl.BlockSpec((tm, D), lambda i: (i, 0)),
            _const_spec((1, D)),
            _const_spec((D, 2 * W)),
            _const_spec((1, W)),
            _const_spec((1, W)),
            _const_spec((GMLP_GROUPS, GMLP_CHUNK, GMLP_CHUNK)),
            _const_spec((GMLP_CHUNK, GMLP_GROUPS)),
            _const_spec((W, D)),
        ],
        out_specs=pl.BlockSpec((tm, D), lambda i: (i, 0)),
        compiler_params=_params("parallel"),
        name="gmlp_mixer",
    )(h, nw.reshape(1, D), w_in.astype(BF16), ln_g.reshape(1, W), ln_b.reshape(1, W), w_s, b_s.T,
      w_out.astype(BF16))


def _router_kernel(h_ref, nw_ref, wrt_ref, meta_ref, gate_ref, cnt_ref, cnt_scr):
    tm = h_ref.shape[0]

    @pl.when(pl.program_id(0) == 0)
    def _():
        cnt_scr[...] = jnp.zeros(cnt_scr.shape, F32)

    xn = _rmsnorm(h_ref[...], nw_ref[...])
    logits = lax.dot_general(wrt_ref[...], xn, (((1,), (1,)), ((), ())),
                             precision=lax.Precision.HIGHEST, preferred_element_type=F32)
    eid = lax.broadcasted_iota(I32, logits.shape, 0).astype(F32)
    m1 = jnp.max(logits, axis=0, keepdims=True)
    e1 = jnp.min(jnp.where(logits == m1, eid, float(N_EXPERTS)), axis=0, keepdims=True)
    rest = jnp.where(eid == e1, -jnp.inf, logits)
    m2 = jnp.max(rest, axis=0, keepdims=True)
    e2 = jnp.min(jnp.where(rest == m2, eid, float(N_EXPERTS)), axis=0, keepdims=True)
    ex = jnp.exp(m2 - m1)
    g1 = 1.0 / (1.0 + ex)
    g2 = ex / (1.0 + ex)
    hit1 = eid == e1
    hit2 = eid == e2
    onehot = jnp.where(hit1 | hit2, 1.0, 0.0)
    row = lax.broadcasted_iota(I32, (tm, tm), 0)
    col = lax.broadcasted_iota(I32, (tm, tm), 1)
    earlier = jnp.where(row < col, 1.0, 0.0).astype(BF16)
    before = _dot(onehot.astype(BF16), earlier) + cnt_scr[...]
    r1 = jnp.sum(jnp.where(hit1, before, 0.0), axis=0, keepdims=True)
    r2 = jnp.sum(jnp.where(hit2, before, 0.0), axis=0, keepdims=True)
    cnt_scr[...] += jnp.sum(onehot, axis=1, keepdims=True)
    zero = jnp.zeros((SUBLANES - 4, tm), F32)
    meta_ref[...] = jnp.concatenate([e1, e2, r1, r2, zero], axis=0).astype(I32)
    gate_ref[...] = jnp.concatenate([g1, g2, zero, 0.0 * g1, 0.0 * g2], axis=0)
    cnt_ref[...] = jnp.broadcast_to(cnt_scr[...], cnt_ref.shape)


def _router(h, nw, w_router):
    T, D = h.shape
    assert N_EXPERTS == SUBLANES
    tm = min(TM_ROUTE, T)
    tok_spec = pl.BlockSpec((SUBLANES, tm), lambda i: (0, i))
    return pl.pallas_call(
        _router_kernel,
        out_shape=(jax.ShapeDtypeStruct((SUBLANES, T), I32), jax.ShapeDtypeStruct((SUBLANES, T), F32),
                   jax.ShapeDtypeStruct((N_EXPERTS, LANES), F32)),
        grid=(T // tm,),
        in_specs=[pl.BlockSpec((tm, D), lambda i: (i, 0)), _const_spec((1, D)), _const_spec((N_EXPERTS, D))],
        out_specs=(tok_spec, tok_spec, _const_spec((N_EXPERTS, LANES))),
        scratch_shapes=[pltpu.VMEM((N_EXPERTS, 1), F32)],
        compiler_params=_params("arbitrary"),
        name="moe_router",
    )(h, nw.reshape(1, D), w_router.T)


ZERO_ROWS = 512


def _dispatch_kernel(zs_ref, zl_ref, d1_ref, d2_ref, p1_ref, p2_ref, h_ref, xs_ref, rows_scr, zero_scr,
                     sem, zsem):
    tm = h_ref.shape[0]
    step = pl.program_id(0)
    slot = step % 2
    _tile_to_rows(rows_scr.at[slot], h_ref[...], tm)

    def copies(dests, buf, r):
        src = rows_scr.at[buf, pl.ds(pl.multiple_of(r * SUBLANES, SUBLANES), SUBLANES), :]
        return [pltpu.make_async_copy(
            src, xs_ref.at[pl.ds(pl.multiple_of(d[0, 0, r] * SUBLANES, SUBLANES), SUBLANES), :],
            sem.at[buf]) for d in dests]

    def start(r, c):
        for prio, cp in enumerate(copies((d1_ref, d2_ref), slot, r)):
            cp.start(priority=prio)
        return c

    def wait_prev(r, c):
        for cp in copies((p1_ref, p2_ref), 1 - slot, r):
            cp.wait()
        return c

    def wait_own(r, c):
        for cp in copies((d1_ref, d2_ref), slot, r):
            cp.wait()
        return c

    def zero_copies(fn):
        def zero_dma(off, size, pred):
            cp = pltpu.make_async_copy(
                zero_scr.at[pl.ds(0, size * SUBLANES), :],
                xs_ref.at[pl.ds(pl.multiple_of(off * SUBLANES, SUBLANES), size * SUBLANES), :], zsem)
            pl.when(pred)(functools.partial(fn, cp))

        for k in range(N_EXPERTS + 1):
            tail = k == N_EXPERTS
            max_rows = 2 * ZERO_ROWS * (N_EXPERTS if tail else 1)
            n = zl_ref[k]
            whole = n // ZERO_ROWS
            for c in range(max_rows // ZERO_ROWS):
                zero_dma(zs_ref[k] + c * ZERO_ROWS, ZERO_ROWS, c < whole)
            base = zs_ref[k] + whole * ZERO_ROWS
            rest = n % ZERO_ROWS
            size = ZERO_ROWS // 2
            while size >= 1:
                zero_dma(base + (rest // (2 * size)) * (2 * size), size, (rest // size) % 2 == 1)
                size //= 2

    first = step == 0

    @pl.when(first)
    def _():
        zero_scr[...] = jnp.zeros(zero_scr.shape, F32)
        zero_copies(lambda cp: cp.start())

    lax.fori_loop(0, tm, start, 0, unroll=4)

    @pl.when(step > 0)
    def _():
        lax.fori_loop(0, tm, wait_prev, 0, unroll=4)

    @pl.when(step == pl.num_programs(0) - 1)
    def _():
        lax.fori_loop(0, tm, wait_own, 0, unroll=4)

    @pl.when(first)
    def _():
        zero_copies(lambda cp: cp.wait())


def _dispatch(h, d1, d2, zero_start, zero_len, P):
    T, D = h.shape
    tm = min(TM_DISPATCH, T)
    nb = T // tm
    smem = functools.partial(pl.BlockSpec, memory_space=pltpu.SMEM)
    grid_spec = pltpu.PrefetchScalarGridSpec(
        num_scalar_prefetch=2,
        grid=(nb,),
        in_specs=[
            smem((1, 1, tm), lambda i, zs, zl: (i, 0, 0)),
            smem((1, 1, tm), lambda i, zs, zl: (i, 0, 0)),
            smem((1, 1, tm), lambda i, zs, zl: (jnp.maximum(i - 1, 0), 0, 0)),
            smem((1, 1, tm), lambda i, zs, zl: (jnp.maximum(i - 1, 0), 0, 0)),
            pl.BlockSpec((tm, D), lambda i, zs, zl: (i, 0)),
        ],
        out_specs=pl.BlockSpec(memory_space=pl.ANY),
        scratch_shapes=[pltpu.VMEM((2, tm * SUBLANES, LANES), F32),
                        pltpu.VMEM((ZERO_ROWS * SUBLANES, LANES), F32),
                        pltpu.SemaphoreType.DMA((2,)), pltpu.SemaphoreType.DMA],
    )
    d1 = d1.reshape(nb, 1, tm)
    d2 = d2.reshape(nb, 1, tm)
    return pl.pallas_call(
        _dispatch_kernel,
        out_shape=jax.ShapeDtypeStruct((P * SUBLANES, LANES), F32),
        grid_spec=grid_spec,
        compiler_params=_params("arbitrary"),
        name="moe_dispatch",
    )(zero_start, zero_len, d1, d2, d1, d2, h)


def _combine_kernel(d1_ref, d2_ref, n1_ref, n2_ref, h_ref, gate_ref, ys_ref, *rest, final_norm):
    if final_norm:
        fw_ref, o_ref, buf1, buf2, sem = rest
    else:
        o_ref, buf1, buf2, sem = rest
    tm = h_ref.shape[0]
    step = pl.program_id(0)
    slot = step % 2

    def copies(srcs, buf, r):
        dst = pl.ds(pl.multiple_of(r * SUBLANES, SUBLANES), SUBLANES)
        return [pltpu.make_async_copy(
            ys_ref.at[pl.ds(pl.multiple_of(d[0, 0, r] * SUBLANES, SUBLANES), SUBLANES), :],
            b.at[buf, dst, :], sem.at[buf]) for d, b in zip(srcs, (buf1, buf2))]

    def issue(srcs, buf):
        def start(r, c):
            for prio, cp in enumerate(copies(srcs, buf, r)):
                cp.start(priority=prio)
            return c

        lax.fori_loop(0, tm, start, 0, unroll=4)

    @pl.when(step == 0)
    def _():
        issue((d1_ref, d2_ref), slot)

    @pl.when(step + 1 < pl.num_programs(0))
    def _():
        issue((n1_ref, n2_ref), 1 - slot)

    def wait(r, c):
        for cp in copies((d1_ref, d2_ref), slot, r):
            cp.wait()
        return c

    lax.fori_loop(0, tm, wait, 0, unroll=4)
    gates = gate_ref[...]
    out = (h_ref[...] + gates[:, 0:1] * _rows_to_tile(buf1.at[slot], tm)
           + gates[:, 1:2] * _rows_to_tile(buf2.at[slot], tm))
    if final_norm:
        out = _rmsnorm(out, fw_ref[...])
    o_ref[...] = out


def _combine(h, gates, ys, d1, d2, final_w=None):
    T, D = h.shape
    tm = min(TM_COMBINE, T)
    nb = T // tm
    smem = functools.partial(pl.BlockSpec, memory_space=pltpu.SMEM)
    in_specs = [
        smem((1, 1, tm), lambda i: (i, 0, 0)),
        smem((1, 1, tm), lambda i: (i, 0, 0)),
        smem((1, 1, tm), lambda i: (jnp.minimum(i + 1, nb - 1), 0, 0)),
        smem((1, 1, tm), lambda i: (jnp.minimum(i + 1, nb - 1), 0, 0)),
        pl.BlockSpec((tm, D), lambda i: (i, 0)),
        pl.BlockSpec((tm, TOP_K), lambda i: (i, 0)),
        pl.BlockSpec(memory_space=pl.ANY),
    ]
    d1 = d1.reshape(nb, 1, tm)
    d2 = d2.reshape(nb, 1, tm)
    args = [d1, d2, d1, d2, h, gates, ys]
    if final_w is not None:
        in_specs.append(_const_spec((1, D)))
        args.append(final_w.reshape(1, D))
    buf = pltpu.VMEM((2, tm * SUBLANES, LANES), F32)
    return pl.pallas_call(
        functools.partial(_combine_kernel, final_norm=final_w is not None),
        out_shape=jax.ShapeDtypeStruct((T, D), F32),
        grid=(nb,),
        in_specs=in_specs,
        out_specs=pl.BlockSpec((tm, D), lambda i: (i, 0)),
        scratch_shapes=[buf, buf, pltpu.SemaphoreType.DMA((2,))],
        compiler_params=_params("arbitrary"),
        name="moe_combine",
    )(*args)


def _moe_layer(h, nw, w_router, wg, wu, wd, layer, final_w=None):
    T, D = h.shape
    meta, gates, counts = _router(h, nw, w_router)
    tm = min(TM_MOE, T)
    sizes = counts[:, 0].astype(I32)
    padded = (sizes + tm - 1) // tm * tm
    pad_end = jnp.cumsum(padded)
    pad_start = pad_end - padded
    d1 = pad_start[meta[0]] + meta[2]
    d2 = pad_start[meta[1]] + meta[3]
    n_tiles = (T * TOP_K + N_EXPERTS * (tm - 1)) // tm
    tile_start = jnp.arange(n_tiles, dtype=I32) * tm
    tile_expert = jnp.minimum(jnp.sum(pad_end[None, :] <= tile_start[:, None], axis=1), N_EXPERTS - 1)
    n_valid = (pad_end[-1] // tm).reshape(1)
    assert tm <= 2 * ZERO_ROWS
    total = n_tiles * tm
    zero_start = jnp.concatenate([pad_start + sizes, pad_end[-1:]])
    zero_len = jnp.concatenate([padded - sizes, total - pad_end[-1:]])
    xs = _dispatch(h, d1, d2, zero_start, zero_len, total)
    ys = _ffn_moe(xs, nw, wg, wu, wd, layer, tile_expert.astype(I32), n_valid.astype(I32), tm)
    return _combine(h, gates[:TOP_K].T, ys, d1, d2, final_w)


def kernel(x, positions, norm_mix, norm_ffn, norm_final, mla_w_in, mla_q_norm, mla_kv_norm, mla_w_uq,
           mla_w_ukv, mla_w_o, lru_w_in, lru_conv_w, lru_conv_b, lru_w_a, lru_b_a, lru_w_x, lru_b_x,
           lru_lam, lru_w_out, pool_w_grp, pool_scale, gmlp_w_in, gmlp_ln_g, gmlp_ln_b, gmlp_w_s,
           gmlp_b_s, gmlp_w_out, ffn_w_gate, ffn_w_up, ffn_w_down, moe_w_router, moe_w_gate, moe_w_up,
           moe_w_down):
    B, S, D = x.shape
    T = B * S
    depth = norm_mix.shape[0]
    h = x.reshape(T, D)
    for i in range(depth):
        m, j = i % 4, i // 4
        if m == 0:
            q, k, v = _mla_proj(h, positions.reshape(1, T), norm_mix[i], mla_w_in[j], mla_q_norm[j],
                                mla_kv_norm[j], mla_w_uq[j], mla_w_ukv[j], B, S)
            o = _attention(q, k, v)
            h = _matmul_residual(o.reshape(T, -1), mla_w_o[j], h)
        elif m == 1:
            h = _lru_mixer(h.reshape(B, S, D), norm_mix[i], lru_w_in[j], lru_conv_w[j], lru_conv_b[j],
                           lru_w_a[j], lru_b_a[j], lru_w_x[j], lru_b_x[j], lru_lam[j],
                           lru_w_out[j]).reshape(T, D)
        elif m == 2:
            h = _pool_mixer(h, norm_mix[i], pool_w_grp[j], pool_scale[j], S)
        else:
            h = _gmlp_mixer(h, norm_mix[i], gmlp_w_in[j], gmlp_ln_g[j], gmlp_ln_b[j], gmlp_w_s[j],
                            gmlp_b_s[j], gmlp_w_out[j])
        kk = i // 2
        if i % 2 == 0:
            h = _ffn_dense(h, norm_ffn[i], ffn_w_gate, ffn_w_up, ffn_w_down, kk)
        else:
            last = i == depth - 1
            h = _moe_layer(h, norm_ffn[i], moe_w_router[kk], moe_w_gate, moe_w_up, moe_w_down, kk,
                           norm_final if last else None)
    if depth % 2 == 1:
        raise NotImplementedError("final RMSNorm is fused into the last MoE combine")
    return h.reshape(B, S, D)
```

```python
import functools

import jax
import jax.numpy as jnp
from jax import lax
from jax.experimental import pallas as pl
from jax.experimental.pallas import tpu as pltpu

F32 = jnp.float32
BF16 = jnp.bfloat16
I32 = jnp.int32

RMS_EPS = 1e-6
LN_EPS = 1e-5

MLA_HEADS = 16
MLA_Q_RANK = 384
MLA_KV_RANK = 256
MLA_NOPE = 64
MLA_ROPE = 32
MLA_V = 64
ROPE_BASE = 10000.0
LRU_HEADS = 8
CONV_WIDTH = 4
LRU_C = 8.0
POOL_WINDOWS = (2, 4, 8, 16)
GMLP_GROUPS = 4
GMLP_CHUNK = 128
N_EXPERTS = 8
TOP_K = 2

LANES = 128
SUBLANES = 8
VMEM_LIMIT_BYTES = 56 * 1024 * 1024

TM_PROJ = 512
TQ_ATTN = 512
HEADS_PER_STEP = 4
SUM_ROWS = 16
TM_RES = 512
TM_FFN = 1024
TF_FFN = 512
ROW_CHUNK = 256
TT_LRU = 512
TM_POOL = 512
TM_GMLP = 512
TM_ROUTE = 512
TM_MOE = 1024
TM_DISPATCH = 512
TM_COMBINE = 256
POOL_HALO = 16


def _params(*sem):
    return pltpu.CompilerParams(dimension_semantics=sem, vmem_limit_bytes=VMEM_LIMIT_BYTES)


def _rmsnorm(x, g):
    return x * lax.rsqrt(jnp.mean(x * x, axis=-1, keepdims=True) + RMS_EPS) * g


def _dot(a, b):
    return jnp.dot(a, b, preferred_element_type=F32)


def _const_spec(shape):
    return pl.BlockSpec(shape, lambda *_: (0,) * len(shape))


def _mla_proj_kernel(h_ref, pos_ref, nw_ref, win_ref, qn_ref, kvn_ref, wuqt_ref, wuk_ref, wuvt_ref,
                     invf_ref, q_ref, k_ref, v_ref, *, scale):
    half = MLA_ROPE // 2
    x = h_ref[...]
    tm = x.shape[0]
    xn = _rmsnorm(x, nw_ref[...]).astype(BF16)
    lat = _dot(xn, win_ref[...])
    cq = _rmsnorm(lat[:, :MLA_Q_RANK], qn_ref[...]).astype(BF16)
    ckv = _rmsnorm(lat[:, MLA_Q_RANK:MLA_Q_RANK + MLA_KV_RANK], kvn_ref[...]).astype(BF16)
    kr = lat[:, MLA_Q_RANK + MLA_KV_RANK:]
    ang = invf_ref[...] * pos_ref[...].astype(F32)
    c = jnp.cos(ang)
    s = jnp.sin(ang)
    pad_r = LANES - MLA_NOPE - MLA_ROPE
    one = jnp.ones((MLA_NOPE, tm), F32)
    c_t = jnp.concatenate([one, c, c, jnp.ones((pad_r, tm), F32)], axis=0).T
    s_lo = jnp.concatenate([0.0 * one, -s, jnp.zeros((half + pad_r, tm), F32)], axis=0).T
    s_hi = jnp.concatenate([0.0 * one, 0.0 * s, s, jnp.zeros((pad_r, tm), F32)], axis=0).T
    kr = kr * c_t + pltpu.roll(kr, LANES - half, 1) * s_lo + pltpu.roll(kr, half, 1) * s_hi
    kn = _dot(ckv, wuk_ref[...])
    for hh in range(MLA_HEADS):
        k_ref[0, hh] = (kn[:, hh * LANES:(hh + 1) * LANES] + kr).astype(BF16)
    nt = (((1,), (1,)), ((), ()))
    qt = lax.dot_general(wuqt_ref[...], cq, nt, preferred_element_type=F32)
    for hh in range(MLA_HEADS):
        blk = qt[hh * LANES:(hh + 1) * LANES]
        x1 = blk[MLA_NOPE:MLA_NOPE + half]
        x2 = blk[MLA_NOPE + half:MLA_NOPE + MLA_ROPE]
        roped = jnp.concatenate(
            [blk[:MLA_NOPE], x1 * c - x2 * s, x2 * c + x1 * s, blk[MLA_NOPE + MLA_ROPE:]], axis=0)
        q_ref[0, hh] = (roped * scale).astype(BF16)
    v_ref[0, 0] = lax.dot_general(wuvt_ref[...], ckv, nt, preferred_element_type=F32).astype(BF16)


def _mla_proj(h, pos, nw, w_in, q_norm, kv_norm, w_uq, w_ukv, B, S):
    T, D = h.shape
    H = MLA_HEADS
    tm = min(TM_PROJ, S)
    ns = S // tm
    qk = MLA_NOPE + MLA_ROPE
    pad_r = LANES - qk
    w_in_p = jnp.concatenate([
        w_in[:, :MLA_Q_RANK + MLA_KV_RANK],
        jnp.zeros((D, MLA_NOPE), F32), w_in[:, MLA_Q_RANK + MLA_KV_RANK:], jnp.zeros((D, pad_r), F32)],
        axis=1).astype(BF16)
    wq = w_uq.reshape(MLA_Q_RANK, H, qk)
    wq_t = jnp.pad(wq, ((0, 0), (0, 0), (0, pad_r))).reshape(MLA_Q_RANK, H * LANES).T.astype(BF16)
    wkv = w_ukv.reshape(MLA_KV_RANK, H, MLA_NOPE + MLA_V)
    wk_p = jnp.pad(wkv[:, :, :MLA_NOPE], ((0, 0), (0, 0), (0, LANES - MLA_NOPE)))
    wk_p = wk_p.reshape(MLA_KV_RANK, H * LANES).astype(BF16)
    wv_t = wkv[:, :, MLA_NOPE:].reshape(MLA_KV_RANK, H * MLA_V).T.astype(BF16)
    half = MLA_ROPE // 2
    invf = (ROPE_BASE ** (-jnp.arange(half, dtype=F32) / half)).reshape(half, 1)
    wl = w_in_p.shape[1]
    out_shape = (
        jax.ShapeDtypeStruct((B, H, LANES, S), BF16),
        jax.ShapeDtypeStruct((B, H, S, LANES), BF16),
        jax.ShapeDtypeStruct((B, ns, H * MLA_V, tm), BF16),
    )
    scale = qk ** -0.5 * 1.4426950408889634
    return pl.pallas_call(
        functools.partial(_mla_proj_kernel, scale=scale),
        out_shape=out_shape,
        grid=(T // tm,),
        in_specs=[
            pl.BlockSpec((tm, D), lambda i: (i, 0)),
            pl.BlockSpec((1, tm), lambda i: (0, i)),
            _const_spec((1, D)),
            _const_spec((D, wl)),
            _const_spec((1, MLA_Q_RANK)),
            _const_spec((1, MLA_KV_RANK)),
            _const_spec((H * LANES, MLA_Q_RANK)),
            _const_spec((MLA_KV_RANK, H * LANES)),
            _const_spec((H * MLA_V, MLA_KV_RANK)),
            _const_spec((half, 1)),
        ],
        out_specs=(pl.BlockSpec((1, H, LANES, tm), lambda i: (i // ns, 0, 0, i % ns)),
                   pl.BlockSpec((1, H, tm, LANES), lambda i: (i // ns, 0, i % ns, 0)),
                   pl.BlockSpec((1, 1, H * MLA_V, tm), lambda i: (i // ns, i % ns, 0, 0))),
        compiler_params=_params("parallel"),
        name="mla_proj",
    )(h, pos, nw.reshape(1, D), w_in_p, q_norm.reshape(1, -1), kv_norm.reshape(1, -1), wq_t, wk_p, wv_t,
      invf)


def _attn_kernel(q_ref, k_ref, v_ref, o_ref, m_scr, acc_scr, s_scr, *, tq):
    qi = pl.program_id(2)
    nh = q_ref.shape[1]
    m_scr[...] = jnp.full(m_scr.shape, -jnp.inf, F32)
    acc_scr[...] = jnp.zeros(acc_scr.shape, F32)

    def scores(j, hh):
        return _dot(k_ref[0, hh, pl.ds(pl.multiple_of(j * tq, tq), tq), :], q_ref[0, hh])

    ones = jnp.ones((SUM_ROWS, tq), BF16)

    def softmax_pv(s, j, hh, masked):
        if masked:
            kv = lax.broadcasted_iota(I32, s.shape, 0)
            qq = lax.broadcasted_iota(I32, s.shape, 1)
            s = jnp.where(kv <= qq, s, -jnp.inf)
        m_prev = m_scr[hh]
        m_new = jnp.maximum(m_prev, jnp.max(s, axis=0, keepdims=True))
        alpha = jnp.exp2(m_prev - m_new)
        p = jnp.exp2(s - m_new).astype(BF16)
        vt = jnp.concatenate([v_ref[0, j, hh * MLA_V:(hh + 1) * MLA_V, :], ones], axis=0)
        acc_scr[hh] = alpha * acc_scr[hh] + _dot(vt, p)
        m_scr[hh] = m_new

    def stage(j, slot, masked, prefetch):
        for hh in range(nh):
            s = s_scr[slot, hh]
            if prefetch:
                s_scr[1 - slot, hh] = scores(j + 1, hh)
            softmax_pv(s, j, hh, masked)

    for hh in range(nh):
        s_scr[0, hh] = scores(0, hh)

    def pair(i, carry):
        stage(2 * i, 0, False, True)
        stage(2 * i + 1, 1, False, True)
        return carry

    lax.fori_loop(0, qi // 2, pair, 0)

    @pl.when(qi % 2 == 1)
    def _():
        stage(qi - 1, 0, False, True)
        stage(qi, 1, True, False)

    @pl.when(qi % 2 == 0)
    def _():
        stage(qi, 0, True, False)
    out = jnp.concatenate([acc_scr[hh, :MLA_V] / acc_scr[hh, MLA_V:MLA_V + 1] for hh in range(nh)], axis=0)
    o_ref[0] = out.T.astype(BF16)


def _attention(qt, k, vt):
    B, H, S, _ = k.shape
    tq = vt.shape[3]
    nh = HEADS_PER_STEP
    return pl.pallas_call(
        functools.partial(_attn_kernel, tq=tq),
        out_shape=jax.ShapeDtypeStruct((B, S, H * MLA_V), BF16),
        grid=(B, H // nh, S // tq),
        in_specs=[
            pl.BlockSpec((1, nh, LANES, tq), lambda b, hp, qi: (b, hp, 0, qi)),
            pl.BlockSpec((1, nh, S, LANES), lambda b, hp, qi: (b, hp, 0, 0)),
            pl.BlockSpec((1, S // tq, nh * MLA_V, tq), lambda b, hp, qi: (b, 0, hp, 0)),
        ],
        out_specs=pl.BlockSpec((1, tq, nh * MLA_V), lambda b, hp, qi: (b, qi, hp)),
        scratch_shapes=[pltpu.VMEM((nh, 1, tq), F32), pltpu.VMEM((nh, MLA_V + SUM_ROWS, tq), F32),
                        pltpu.VMEM((2, nh, tq, tq), F32)],
        compiler_params=_params("parallel", "parallel", "arbitrary"),
        name="mla_attention",
    )(qt, k, vt)


def _matmul_res_kernel(x_ref, w_ref, h_ref, o_ref):
    o_ref[...] = h_ref[...] + _dot(x_ref[...], w_ref[...])


def _matmul_residual(x, w, h):
    T, K = x.shape
    N = w.shape[1]
    tm = min(TM_RES, T)
    return pl.pallas_call(
        _matmul_res_kernel,
        out_shape=jax.ShapeDtypeStruct((T, N), F32),
        grid=(T // tm,),
        in_specs=[pl.BlockSpec((tm, K), lambda i: (i, 0)), _const_spec((K, N)),
                  pl.BlockSpec((tm, N), lambda i: (i, 0))],
        out_specs=pl.BlockSpec((tm, N), lambda i: (i, 0)),
        compiler_params=_params("parallel"),
        name="matmul_residual",
    )(x, w.astype(BF16), h)


def _swiglu(xn, wg, wu, wd):
    g = _dot(xn, wg)
    u = _dot(xn, wu)
    hmid = (g * jax.nn.sigmoid(g) * u).astype(BF16)
    return _dot(hmid, wd)


def _swiglu_steps(load_x, nw_ref, wg_ref, wu_ref, wd_ref, xn_scr, acc_scr, rows=None):
    j = pl.program_id(1)
    tm = xn_scr.shape[0]

    @pl.when(j == 0)
    def _():
        xn_scr[...] = _rmsnorm(load_x(), nw_ref[...]).astype(BF16)
        acc_scr[...] = jnp.zeros(acc_scr.shape, F32)

    def weights():
        return wg_ref[...].astype(BF16), wu_ref[...].astype(BF16), wd_ref[...].astype(BF16)

    def whole():
        acc_scr[...] += _swiglu(xn_scr[...], *weights())

    if rows is None:
        whole()
        return
    pl.when(rows > tm - ROW_CHUNK)(whole)

    @pl.when(rows <= tm - ROW_CHUNK)
    def _():
        w = weights()

        def chunk(c, carry):
            r = pl.ds(pl.multiple_of(c * ROW_CHUNK, ROW_CHUNK), ROW_CHUNK)
            acc_scr[r, :] += _swiglu(xn_scr[r, :], *w)
            return carry

        lax.fori_loop(0, (rows + ROW_CHUNK - 1) // ROW_CHUNK, chunk, 0)


def _ffn_dense_kernel(x_ref, nw_ref, wg_ref, wu_ref, wd_ref, o_ref, xn_scr, acc_scr):
    _swiglu_steps(lambda: x_ref[...], nw_ref, wg_ref, wu_ref, wd_ref, xn_scr, acc_scr)

    @pl.when(pl.program_id(1) == pl.num_programs(1) - 1)
    def _():
        o_ref[...] = x_ref[...] + acc_scr[...]


def _ffn_dense(h, nw, wg, wu, wd, layer):
    T, D = h.shape
    FF = wg.shape[2]
    tm = min(TM_FFN, T)
    tf = min(TF_FFN, FF)
    return pl.pallas_call(
        _ffn_dense_kernel,
        out_shape=jax.ShapeDtypeStruct((T, D), F32),
        grid=(T // tm, FF // tf),
        in_specs=[
            pl.BlockSpec((tm, D), lambda i, j: (i, 0)),
            _const_spec((1, D)),
            pl.BlockSpec((None, D, tf), lambda i, j: (layer, 0, j)),
            pl.BlockSpec((None, D, tf), lambda i, j: (layer, 0, j)),
            pl.BlockSpec((None, tf, D), lambda i, j: (layer, j, 0)),
        ],
        out_specs=pl.BlockSpec((tm, D), lambda i, j: (i, 0)),
        scratch_shapes=[pltpu.VMEM((tm, D), BF16), pltpu.VMEM((tm, D), F32)],
        compiler_params=_params("parallel", "arbitrary"),
        name="ffn_dense",
    )(h, nw.reshape(1, D), wg, wu, wd)


def _rows_to_tile(ref, tm):
    return jnp.concatenate([ref[pl.ds(s, tm, stride=SUBLANES), :] for s in range(SUBLANES)], axis=1)


def _tile_to_rows(ref, x, tm):
    for s in range(SUBLANES):
        ref[pl.ds(s, tm, stride=SUBLANES), :] = x[:, s * LANES:(s + 1) * LANES]


def _ffn_moe_kernel(te_ref, nv_ref, tr_ref, x_ref, nw_ref, wg_ref, wu_ref, wd_ref, o_ref, xn_scr, acc_scr):
    del te_ref
    tm = xn_scr.shape[0]
    valid = pl.program_id(0) < nv_ref[0]
    last = pl.program_id(1) == pl.num_programs(1) - 1

    @pl.when(valid)
    def _():
        _swiglu_steps(lambda: _rows_to_tile(x_ref, tm), nw_ref, wg_ref, wu_ref, wd_ref, xn_scr, acc_scr,
                      rows=tr_ref[pl.program_id(0)])

        @pl.when(last)
        def _():
            _tile_to_rows(o_ref, acc_scr[...], tm)

    @pl.when(jnp.logical_and(jnp.logical_not(valid), last))
    def _():
        o_ref[...] = jnp.zeros(o_ref.shape, F32)


def _ffn_moe(xs, nw, wg, wu, wd, layer, tile_expert, n_valid, tile_rows, tm):
    D = wg.shape[2]
    FF = wg.shape[3]
    P = xs.shape[0] // SUBLANES
    tf = min(TF_FFN, FF)
    nj = FF // tf

    def tile(i, nv):
        return jnp.maximum(jnp.minimum(i, nv[0] - 1), 0)

    def col(i, j, nv):
        return jnp.where(i < nv[0], j, nj - 1)

    def up_map(i, j, te, nv, tr):
        return (layer, te[tile(i, nv)], 0, col(i, j, nv))

    def down_map(i, j, te, nv, tr):
        return (layer, te[tile(i, nv)], col(i, j, nv), 0)

    grid_spec = pltpu.PrefetchScalarGridSpec(
        num_scalar_prefetch=3,
        grid=(P // tm, nj),
        in_specs=[
            pl.BlockSpec((tm * SUBLANES, LANES), lambda i, j, te, nv, tr: (tile(i, nv), 0)),
            pl.BlockSpec((1, D), lambda i, j, te, nv, tr: (0, 0)),
            pl.BlockSpec((None, None, D, tf), up_map),
            pl.BlockSpec((None, None, D, tf), up_map),
            pl.BlockSpec((None, None, tf, D), down_map),
        ],
        out_specs=pl.BlockSpec((tm * SUBLANES, LANES), lambda i, j, te, nv, tr: (i, 0)),
        scratch_shapes=[pltpu.VMEM((tm, D), BF16), pltpu.VMEM((tm, D), F32)],
    )
    return pl.pallas_call(
        _ffn_moe_kernel,
        out_shape=jax.ShapeDtypeStruct((P * SUBLANES, LANES), F32),
        grid_spec=grid_spec,
        compiler_params=_params("arbitrary", "arbitrary"),
        name="ffn_moe",
    )(tile_expert, n_valid, tile_rows, xs, nw.reshape(1, D), wg, wu, wd)


def _lru_kernel(h_ref, nw_ref, win_ref, cw_ref, cb_ref, wax_ref, ba_ref, bx_ref, lam_ref, wout_ref,
                o_ref, halo_scr, carry_scr, a_scr, b_scr, hs_scr):
    tt, W = h_ref.shape[1], a_scr.shape[2]
    G = tt // SUBLANES

    @pl.when(pl.program_id(1) == 0)
    def _():
        halo_scr[...] = jnp.zeros(halo_scr.shape, F32)
        carry_scr[...] = jnp.zeros(carry_scr.shape, F32)

    x = h_ref[0]
    xn = _rmsnorm(x, nw_ref[...]).astype(BF16)
    z = _dot(xn, win_ref[...])
    gate = jax.nn.gelu(z[:, :W])
    xb = z[:, W:]
    xe = jnp.concatenate([halo_scr[...], xb], axis=0)
    halo_scr[...] = xb[tt - SUBLANES:, :]
    cw = cw_ref[...]
    xc = cb_ref[...]
    for kk in range(CONV_WIDTH):
        off = SUBLANES - (CONV_WIDTH - 1) + kk
        xc = xc + cw[kk:kk + 1, :] * xe[off:off + tt, :]
    xcb = xc.astype(BF16)
    hd = W // LRU_HEADS
    r_parts, i_parts = [], []
    for n in range(LRU_HEADS):
        ri = _dot(xcb[:, n * hd:(n + 1) * hd], wax_ref[n])
        r_parts.append(ri[:, :hd])
        i_parts.append(ri[:, hd:])
    r = jax.nn.sigmoid(jnp.concatenate(r_parts, axis=1) + ba_ref[...])
    ig = jax.nn.sigmoid(jnp.concatenate(i_parts, axis=1) + bx_ref[...])
    nl = -lam_ref[...]
    softplus = jnp.maximum(nl, 0.0) + jnp.log1p(jnp.exp(-jnp.abs(nl)))
    a = jnp.exp(-LRU_C * r * softplus)
    b = jnp.sqrt(1.0 - a * a) * (ig * xc)

    A = a.reshape(G, SUBLANES, W)
    Bv = b.reshape(G, SUBLANES, W)
    sub = lax.broadcasted_iota(I32, A.shape, 1)
    for d in (1, 2, 4):
        keep = sub >= d
        a_sh = jnp.where(keep, pltpu.roll(A, d, 1), 1.0)
        b_sh = jnp.where(keep, pltpu.roll(Bv, d, 1), 0.0)
        Bv = A * b_sh + Bv
        A = A * a_sh
    a_scr[...] = A
    b_scr[...] = Bv

    def body(g, hprev):
        hg = b_scr[g] + a_scr[g] * hprev
        hs_scr[g] = hg
        return hg[SUBLANES - 1:, :]

    carry_scr[...] = lax.fori_loop(0, G, body, carry_scr[...], unroll=8)
    y = (hs_scr[...].reshape(tt, W) * gate).astype(BF16)
    o_ref[0] = x + _dot(y, wout_ref[...])


def _lru_mixer(h, nw, w_in, conv_w, conv_b, w_a, b_a, w_x, b_x, lam, w_out):
    B, S, D = h.shape
    W = w_out.shape[0]
    tt = min(TT_LRU, S)
    hd = W // LRU_HEADS
    wax = jnp.concatenate([w_a, w_x], axis=2).astype(BF16)
    G = tt // SUBLANES
    return pl.pallas_call(
        _lru_kernel,
        out_shape=jax.ShapeDtypeStruct((B, S, D), F32),
        grid=(B, S // tt),
        in_specs=[
            pl.BlockSpec((1, tt, D), lambda b, t: (b, t, 0)),
            _const_spec((1, D)),
            _const_spec((D, 2 * W)),
            _const_spec((CONV_WIDTH, W)),
            _const_spec((1, W)),
            _const_spec((LRU_HEADS, hd, 2 * hd)),
            _const_spec((1, W)),
            _const_spec((1, W)),
            _const_spec((1, W)),
            _const_spec((W, D)),
        ],
        out_specs=pl.BlockSpec((1, tt, D), lambda b, t: (b, t, 0)),
        scratch_shapes=[
            pltpu.VMEM((SUBLANES, W), F32), pltpu.VMEM((1, W), F32),
            pltpu.VMEM((G, SUBLANES, W), F32), pltpu.VMEM((G, SUBLANES, W), F32),
            pltpu.VMEM((G, SUBLANES, W), F32),
        ],
        compiler_params=_params("parallel", "arbitrary"),
        name="rglru_mixer",
    )(h, nw.reshape(1, D), w_in.astype(BF16), conv_w, conv_b.reshape(1, W), wax, b_a.reshape(1, W),
      b_x.reshape(1, W), lam.reshape(1, W), w_out.astype(BF16))


def _pool_kernel(h_ref, halo_ref, nw_ref, wg_ref, sc_ref, o_ref, *, blocks_per_seq):
    tm, D = h_ref.shape
    gsz = D // len(POOL_WINDOWS)
    x = h_ref[...]
    first = pl.program_id(0) % blocks_per_seq == 0
    xn = _rmsnorm(x, nw_ref[...])
    hn_halo = _rmsnorm(halo_ref[...], nw_ref[...])
    hn_halo = jnp.where(first, 0.0, hn_halo)
    e = jnp.concatenate([hn_halo, xn], axis=0)
    t = (pl.program_id(0) % blocks_per_seq) * tm + lax.broadcasted_iota(I32, (tm, 1), 0)
    outs = []
    for g, w in enumerate(POOL_WINDOWS):
        cur = e[:, g * gsz:(g + 1) * gsz]
        base = 0
        d = 1
        while d < w:
            cur = cur[d:, :] + cur[:-d, :]
            base += d
            d *= 2
        win = cur[POOL_HALO - base:POOL_HALO - base + tm, :]
        count = jnp.minimum(t + 1, w).astype(F32)
        pooled = (win / count - xn[:, g * gsz:(g + 1) * gsz]).astype(BF16)
        outs.append(_dot(pooled, wg_ref[g]))
    o_ref[...] = x + jnp.concatenate(outs, axis=1) * sc_ref[...]


def _pool_mixer(h, nw, w_grp, scale, S):
    T, D = h.shape
    tm = min(TM_POOL, S)
    bps = S // tm
    hb = tm // POOL_HALO
    ng, gsz = w_grp.shape[0], w_grp.shape[1]
    return pl.pallas_call(
        functools.partial(_pool_kernel, blocks_per_seq=bps),
        out_shape=jax.ShapeDtypeStruct((T, D), F32),
        grid=(T // tm,),
        in_specs=[
            pl.BlockSpec((tm, D), lambda i: (i, 0)),
            pl.BlockSpec((POOL_HALO, D), lambda i: (jnp.maximum(i * hb - 1, 0), 0)),
            _const_spec((1, D)),
            _const_spec((ng, gsz, gsz)),
            _const_spec((1, D)),
        ],
        out_specs=pl.BlockSpec((tm, D), lambda i: (i, 0)),
        compiler_params=_params("parallel"),
        name="pool_mixer",
    )(h, h, nw.reshape(1, D), w_grp.astype(BF16), scale.reshape(1, D))


def _gmlp_kernel(h_ref, nw_ref, win_ref, lg_ref, lb_ref, ws_ref, bs_ref, wout_ref, o_ref):
    tm, D = h_ref.shape
    W = wout_ref.shape[0]
    gw = W // GMLP_GROUPS
    x = h_ref[...]
    xn = _rmsnorm(x, nw_ref[...]).astype(BF16)
    z = jax.nn.gelu(_dot(xn, win_ref[...]))
    u = z[:, :W]
    v = z[:, W:]
    mu = jnp.mean(v, axis=-1, keepdims=True)
    vc = v - mu
    var = jnp.mean(vc * vc, axis=-1, keepdims=True)
    vb = (vc * lax.rsqrt(var + LN_EPS) * lg_ref[...] + lb_ref[...]).astype(BF16)
    row = lax.broadcasted_iota(I32, (GMLP_CHUNK, GMLP_CHUNK), 0)
    col = lax.broadcasted_iota(I32, (GMLP_CHUNK, GMLP_CHUNK), 1)
    bs = bs_ref[...]
    ws = [jnp.where(col <= row, ws_ref[g], 0.0).astype(BF16) for g in range(GMLP_GROUPS)]
    rows = []
    for c in range(tm // GMLP_CHUNK):
        cols = []
        for g in range(GMLP_GROUPS):
            blk = vb[c * GMLP_CHUNK:(c + 1) * GMLP_CHUNK, g * gw:(g + 1) * gw]
            cols.append(_dot(ws[g], blk) + bs[:, g:g + 1])
        rows.append(jnp.concatenate(cols, axis=1))
    v2 = jnp.concatenate(rows, axis=0)
    o_ref[...] = x + _dot((u * v2).astype(BF16), wout_ref[...])


def _gmlp_mixer(h, nw, w_in, ln_g, ln_b, w_s, b_s, w_out):
    T, D = h.shape
    W = w_out.shape[0]
    tm = min(TM_GMLP, T)
    return pl.pallas_call(
        _gmlp_kernel,
        out_shape=jax.ShapeDtypeStruct((T, D), F32),
        grid=(T // tm,),
        in_specs=[
            pl.BlockSpec((tm, D), lambda i: (i, 0)),
            _const_spec((1, D)),
            _const_spec((D, 2 * W)),
            _const_spec((1, W)),
            _const_spec((1, W)),
            _const_spec((GMLP_GROUPS, GMLP_CHUNK, GMLP_CHUNK)),
            _const_spec((GMLP_CHUNK, GMLP_GROUPS)),
            _const_spec((W, D)),
        ],
        out_specs=pl.BlockSpec((tm, D), lambda i: (i, 0)),
        compiler_params=_params("parallel"),
        name="gmlp_mixer",
    )(h, nw.reshape(1, D), w_in.astype(BF16), ln_g.reshape(1, W), ln_b.reshape(1, W), w_s, b_s.T,
      w_out.astype(BF16))


def _router_kernel(h_ref, nw_ref, wrt_ref, meta_ref, gate_ref, cnt_ref, cnt_scr):
    tm = h_ref.shape[0]

    @pl.when(pl.program_id(0) == 0)
    def _():
        cnt_scr[...] = jnp.zeros(cnt_scr.shape, F32)

    xn = _rmsnorm(h_ref[...], nw_ref[...])
    logits = lax.dot_general(wrt_ref[...], xn, (((1,), (1,)), ((), ())),
                             precision=lax.Precision.HIGHEST, preferred_element_type=F32)
    eid = lax.broadcasted_iota(I32, logits.shape, 0).astype(F32)
    m1 = jnp.max(logits, axis=0, keepdims=True)
    e1 = jnp.min(jnp.where(logits == m1, eid, float(N_EXPERTS)), axis=0, keepdims=True)
    rest = jnp.where(eid == e1, -jnp.inf, logits)
    m2 = jnp.max(rest, axis=0, keepdims=True)
    e2 = jnp.min(jnp.where(rest == m2, eid, float(N_EXPERTS)), axis=0, keepdims=True)
    ex = jnp.exp(m2 - m1)
    g1 = 1.0 / (1.0 + ex)
    g2 = ex / (1.0 + ex)
    hit1 = eid == e1
    hit2 = eid == e2
    onehot = jnp.where(hit1 | hit2, 1.0, 0.0)
    row = lax.broadcasted_iota(I32, (tm, tm), 0)
    col = lax.broadcasted_iota(I32, (tm, tm), 1)
    earlier = jnp.where(row < col, 1.0, 0.0).astype(BF16)
    before = _dot(onehot.astype(BF16), earlier) + cnt_scr[...]
    r1 = jnp.sum(jnp.where(hit1, before, 0.0), axis=0, keepdims=True)
    r2 = jnp.sum(jnp.where(hit2, before, 0.0), axis=0, keepdims=True)
    cnt_scr[...] += jnp.sum(onehot, axis=1, keepdims=True)
    zero = jnp.zeros((SUBLANES - 4, tm), F32)
    meta_ref[...] = jnp.concatenate([e1, e2, r1, r2, zero], axis=0).astype(I32)
    gate_ref[...] = jnp.concatenate([g1, g2, zero, 0.0 * g1, 0.0 * g2], axis=0)
    cnt_ref[...] = jnp.broadcast_to(cnt_scr[...], cnt_ref.shape)


def _router(h, nw, w_router):
    T, D = h.shape
    assert N_EXPERTS == SUBLANES
    tm = min(TM_ROUTE, T)
    tok_spec = pl.BlockSpec((SUBLANES, tm), lambda i: (0, i))
    return pl.pallas_call(
        _router_kernel,
        out_shape=(jax.ShapeDtypeStruct((SUBLANES, T), I32), jax.ShapeDtypeStruct((SUBLANES, T), F32),
                   jax.ShapeDtypeStruct((N_EXPERTS, LANES), F32)),
        grid=(T // tm,),
        in_specs=[pl.BlockSpec((tm, D), lambda i: (i, 0)), _const_spec((1, D)), _const_spec((N_EXPERTS, D))],
        out_specs=(tok_spec, tok_spec, _const_spec((N_EXPERTS, LANES))),
        scratch_shapes=[pltpu.VMEM((N_EXPERTS, 1), F32)],
        compiler_params=_params("arbitrary"),
        name="moe_router",
    )(h, nw.reshape(1, D), w_router.T)


ZERO_ROWS = 512


def _dispatch_kernel(zs_ref, zl_ref, d1_ref, d2_ref, h_ref, xs_ref, rows_scr, zero_scr, sem, zsem):
    tm = h_ref.shape[0]
    _tile_to_rows(rows_scr, h_ref[...], tm)

    def copies(r):
        src = rows_scr.at[pl.ds(pl.multiple_of(r * SUBLANES, SUBLANES), SUBLANES), :]
        return [pltpu.make_async_copy(
            src, xs_ref.at[pl.ds(pl.multiple_of(d[0, 0, r] * SUBLANES, SUBLANES), SUBLANES), :], sem)
            for d in (d1_ref, d2_ref)]

    def start(r, c):
        for prio, cp in enumerate(copies(r)):
            cp.start(priority=prio)
        return c

    def wait(r, c):
        for cp in copies(r):
            cp.wait()
        return c

    def zero_copies(fn):
        def zero_dma(off, size, pred):
            cp = pltpu.make_async_copy(
                zero_scr.at[pl.ds(0, size * SUBLANES), :],
                xs_ref.at[pl.ds(pl.multiple_of(off * SUBLANES, SUBLANES), size * SUBLANES), :], zsem)
            pl.when(pred)(functools.partial(fn, cp))

        for k in range(N_EXPERTS + 1):
            tail = k == N_EXPERTS
            max_rows = 2 * ZERO_ROWS * (N_EXPERTS if tail else 1)
            n = zl_ref[k]
            whole = n // ZERO_ROWS
            for c in range(max_rows // ZERO_ROWS):
                zero_dma(zs_ref[k] + c * ZERO_ROWS, ZERO_ROWS, c < whole)
            base = zs_ref[k] + whole * ZERO_ROWS
            rest = n % ZERO_ROWS
            size = ZERO_ROWS // 2
            while size >= 1:
                zero_dma(base + (rest // (2 * size)) * (2 * size), size, (rest // size) % 2 == 1)
                size //= 2

    first = pl.program_id(0) == 0

    @pl.when(first)
    def _():
        zero_scr[...] = jnp.zeros(zero_scr.shape, F32)
        zero_copies(lambda cp: cp.start())

    lax.fori_loop(0, tm, start, 0, unroll=4)
    lax.fori_loop(0, tm, wait, 0, unroll=4)

    @pl.when(first)
    def _():
        zero_copies(lambda cp: cp.wait())


def _dispatch(h, d1, d2, zero_start, zero_len, P):
    T, D = h.shape
    tm = min(TM_DISPATCH, T)
    nb = T // tm
    smem = functools.partial(pl.BlockSpec, memory_space=pltpu.SMEM)
    grid_spec = pltpu.PrefetchScalarGridSpec(
        num_scalar_prefetch=2,
        grid=(nb,),
        in_specs=[
            smem((1, 1, tm), lambda i, zs, zl: (i, 0, 0)),
            smem((1, 1, tm), lambda i, zs, zl: (i, 0, 0)),
            pl.BlockSpec((tm, D), lambda i, zs, zl: (i, 0)),
        ],
        out_specs=pl.BlockSpec(memory_space=pl.ANY),
        scratch_shapes=[pltpu.VMEM((tm * SUBLANES, LANES), F32), pltpu.VMEM((ZERO_ROWS * SUBLANES, LANES), F32),
                        pltpu.SemaphoreType.DMA, pltpu.SemaphoreType.DMA],
    )
    return pl.pallas_call(
        _dispatch_kernel,
        out_shape=jax.ShapeDtypeStruct((P * SUBLANES, LANES), F32),
        grid_spec=grid_spec,
        compiler_params=_params("arbitrary"),
        name="moe_dispatch",
    )(zero_start, zero_len, d1.reshape(nb, 1, tm), d2.reshape(nb, 1, tm), h)


def _combine_kernel(d1_ref, d2_ref, n1_ref, n2_ref, h_ref, gate_ref, ys_ref, *rest, final_norm):
    if final_norm:
        fw_ref, o_ref, buf1, buf2, sem = rest
    else:
        o_ref, buf1, buf2, sem = rest
    tm = h_ref.shape[0]
    step = pl.program_id(0)
    slot = step % 2

    def copies(srcs, buf, r):
        dst = pl.ds(pl.multiple_of(r * SUBLANES, SUBLANES), SUBLANES)
        return [pltpu.make_async_copy(
            ys_ref.at[pl.ds(pl.multiple_of(d[0, 0, r] * SUBLANES, SUBLANES), SUBLANES), :],
            b.at[buf, dst, :], sem.at[buf]) for d, b in zip(srcs, (buf1, buf2))]

    def issue(srcs, buf):
        def start(r, c):
            for prio, cp in enumerate(copies(srcs, buf, r)):
                cp.start(priority=prio)
            return c

        lax.fori_loop(0, tm, start, 0, unroll=4)

    @pl.when(step == 0)
    def _():
        issue((d1_ref, d2_ref), slot)

    @pl.when(step + 1 < pl.num_programs(0))
    def _():
        issue((n1_ref, n2_ref), 1 - slot)

    def wait(r, c):
        for cp in copies((d1_ref, d2_ref), slot, r):
            cp.wait()
        return c

    lax.fori_loop(0, tm, wait, 0, unroll=4)
    gates = gate_ref[...]
    out = (h_ref[...] + gates[:, 0:1] * _rows_to_tile(buf1.at[slot], tm)
           + gates[:, 1:2] * _rows_to_tile(buf2.at[slot], tm))
    if final_norm:
        out = _rmsnorm(out, fw_ref[...])
    o_ref[...] = out


def _combine(h, gates, ys, d1, d2, final_w=None):
    T, D = h.shape
    tm = min(TM_COMBINE, T)
    nb = T // tm
    smem = functools.partial(pl.BlockSpec, memory_space=pltpu.SMEM)
    in_specs = [
        smem((1, 1, tm), lambda i: (i, 0, 0)),
        smem((1, 1, tm), lambda i: (i, 0, 0)),
        smem((1, 1, tm), lambda i: (jnp.minimum(i + 1, nb - 1), 0, 0)),
        smem((1, 1, tm), lambda i: (jnp.minimum(i + 1, nb - 1), 0, 0)),
        pl.BlockSpec((tm, D), lambda i: (i, 0)),
        pl.BlockSpec((tm, TOP_K), lambda i: (i, 0)),
        pl.BlockSpec(memory_space=pl.ANY),
    ]
    d1 = d1.reshape(nb, 1, tm)
    d2 = d2.reshape(nb, 1, tm)
    args = [d1, d2, d1, d2, h, gates, ys]
    if final_w is not None:
        in_specs.append(_const_spec((1, D)))
        args.append(final_w.reshape(1, D))
    buf = pltpu.VMEM((2, tm * SUBLANES, LANES), F32)
    return pl.pallas_call(
        functools.partial(_combine_kernel, final_norm=final_w is not None),
        out_shape=jax.ShapeDtypeStruct((T, D), F32),
        grid=(nb,),
        in_specs=in_specs,
        out_specs=pl.BlockSpec((tm, D), lambda i: (i, 0)),
        scratch_shapes=[buf, buf, pltpu.SemaphoreType.DMA((2,))],
        compiler_params=_params("arbitrary"),
        name="moe_combine",
    )(*args)


def _moe_layer(h, nw, w_router, wg, wu, wd, layer, final_w=None):
    T, D = h.shape
    meta, gates, counts = _router(h, nw, w_router)
    tm = min(TM_MOE, T)
    sizes = counts[:, 0].astype(I32)
    padded = (sizes + tm - 1) // tm * tm
    pad_end = jnp.cumsum(padded)
    pad_start = pad_end - padded
    d1 = pad_start[meta[0]] + meta[2]
    d2 = pad_start[meta[1]] + meta[3]
    n_tiles = (T * TOP_K + N_EXPERTS * (tm - 1)) // tm
    tile_start = jnp.arange(n_tiles, dtype=I32) * tm
    tile_expert = jnp.minimum(jnp.sum(pad_end[None, :] <= tile_start[:, None], axis=1), N_EXPERTS - 1)
    n_valid = (pad_end[-1] // tm).reshape(1)
    assert tm <= 2 * ZERO_ROWS
    total = n_tiles * tm
    zero_start = jnp.concatenate([pad_start + sizes, pad_end[-1:]])
    zero_len = jnp.concatenate([padded - sizes, total - pad_end[-1:]])
    xs = _dispatch(h, d1, d2, zero_start, zero_len, total)
    tile_rows = jnp.clip((pad_start + sizes)[tile_expert] - tile_start, 0, tm)
    ys = _ffn_moe(xs, nw, wg, wu, wd, layer, tile_expert.astype(I32), n_valid.astype(I32),
                  tile_rows.astype(I32), tm)
    return _combine(h, gates[:TOP_K].T, ys, d1, d2, final_w)


def kernel(x, positions, norm_mix, norm_ffn, norm_final, mla_w_in, mla_q_norm, mla_kv_norm, mla_w_uq,
           mla_w_ukv, mla_w_o, lru_w_in, lru_conv_w, lru_conv_b, lru_w_a, lru_b_a, lru_w_x, lru_b_x,
           lru_lam, lru_w_out, pool_w_grp, pool_scale, gmlp_w_in, gmlp_ln_g, gmlp_ln_b, gmlp_w_s,
           gmlp_b_s, gmlp_w_out, ffn_w_gate, ffn_w_up, ffn_w_down, moe_w_router, moe_w_gate, moe_w_up,
           moe_w_down):
    B, S, D = x.shape
    T = B * S
    depth = norm_mix.shape[0]
    h = x.reshape(T, D)
    for i in range(depth):
        m, j = i % 4, i // 4
        if m == 0:
            q, k, v = _mla_proj(h, positions.reshape(1, T), norm_mix[i], mla_w_in[j], mla_q_norm[j],
                                mla_kv_norm[j], mla_w_uq[j], mla_w_ukv[j], B, S)
            o = _attention(q, k, v)
            h = _matmul_residual(o.reshape(T, -1), mla_w_o[j], h)
        elif m == 1:
            h = _lru_mixer(h.reshape(B, S, D), norm_mix[i], lru_w_in[j], lru_conv_w[j], lru_conv_b[j],
                           lru_w_a[j], lru_b_a[j], lru_w_x[j], lru_b_x[j], lru_lam[j],
                           lru_w_out[j]).reshape(T, D)
        elif m == 2:
            h = _pool_mixer(h, norm_mix[i], pool_w_grp[j], pool_scale[j], S)
        else:
            h = _gmlp_mixer(h, norm_mix[i], gmlp_w_in[j], gmlp_ln_g[j], gmlp_ln_b[j], gmlp_w_s[j],
                            gmlp_b_s[j], gmlp_w_out[j])
        kk = i // 2
        if i % 2 == 0:
            h = _ffn_dense(h, norm_ffn[i], ffn_w_gate, ffn_w_up, ffn_w_down, kk)
        else:
            last = i == depth - 1
            h = _moe_layer(h, norm_ffn[i], moe_w_router[kk], moe_w_gate, moe_w_up, moe_w_down, kk,
                           norm_final if last else None)
    if depth % 2 == 1:
        raise NotImplementedError("final RMSNorm is fused into the last MoE combine")
    return h.reshape(B, S, D)
```

```python
import functools

import jax
import jax.numpy as jnp
from jax import lax
from jax.experimental import pallas as pl
from jax.experimental.pallas import tpu as pltpu

F32 = jnp.float32
BF16 = jnp.bfloat16
I32 = jnp.int32

RMS_EPS = 1e-6
LN_EPS = 1e-5

MLA_HEADS = 16
MLA_Q_RANK = 384
MLA_KV_RANK = 256
MLA_NOPE = 64
MLA_ROPE = 32
MLA_V = 64
ROPE_BASE = 10000.0
LRU_HEADS = 8
CONV_WIDTH = 4
LRU_C = 8.0
POOL_WINDOWS = (2, 4, 8, 16)
GMLP_GROUPS = 4
GMLP_CHUNK = 128
N_EXPERTS = 8
TOP_K = 2

LANES = 128
SUBLANES = 8
VMEM_LIMIT_BYTES = 56 * 1024 * 1024

TM_PROJ = 512
TQ_ATTN = 512
HEADS_PER_STEP = 8
SUM_ROWS = 16
TM_RES = 512
TM_FFN = 1024
TF_FFN = 512
ROW_CHUNK = 256
TT_LRU = 512
TM_POOL = 512
TM_GMLP = 512
TM_ROUTE = 512
TM_MOE = 1024
TM_DISPATCH = 512
TM_COMBINE = 512
POOL_HALO = 16


def _params(*sem):
    return pltpu.CompilerParams(dimension_semantics=sem, vmem_limit_bytes=VMEM_LIMIT_BYTES)


def _rmsnorm(x, g):
    return x * lax.rsqrt(jnp.mean(x * x, axis=-1, keepdims=True) + RMS_EPS) * g


def _dot(a, b):
    return jnp.dot(a, b, preferred_element_type=F32)


def _const_spec(shape):
    return pl.BlockSpec(shape, lambda *_: (0,) * len(shape))


def _mla_proj_kernel(h_ref, pos_ref, nw_ref, win_ref, qn_ref, kvn_ref, wuqt_ref, wuk_ref, wuvt_ref,
                     invf_ref, q_ref, k_ref, v_ref, *, scale):
    half = MLA_ROPE // 2
    x = h_ref[...]
    tm = x.shape[0]
    xn = _rmsnorm(x, nw_ref[...]).astype(BF16)
    lat = _dot(xn, win_ref[...])
    cq = _rmsnorm(lat[:, :MLA_Q_RANK], qn_ref[...]).astype(BF16)
    ckv = _rmsnorm(lat[:, MLA_Q_RANK:MLA_Q_RANK + MLA_KV_RANK], kvn_ref[...]).astype(BF16)
    kr = lat[:, MLA_Q_RANK + MLA_KV_RANK:]
    ang = invf_ref[...] * pos_ref[...].astype(F32)
    c = jnp.cos(ang)
    s = jnp.sin(ang)
    pad_r = LANES - MLA_NOPE - MLA_ROPE
    one = jnp.ones((MLA_NOPE, tm), F32)
    c_t = jnp.concatenate([one, c, c, jnp.ones((pad_r, tm), F32)], axis=0).T
    s_lo = jnp.concatenate([0.0 * one, -s, jnp.zeros((half + pad_r, tm), F32)], axis=0).T
    s_hi = jnp.concatenate([0.0 * one, 0.0 * s, s, jnp.zeros((pad_r, tm), F32)], axis=0).T
    kr = kr * c_t + pltpu.roll(kr, LANES - half, 1) * s_lo + pltpu.roll(kr, half, 1) * s_hi
    kn = _dot(ckv, wuk_ref[...])
    for hh in range(MLA_HEADS):
        k_ref[0, hh] = (kn[:, hh * LANES:(hh + 1) * LANES] + kr).astype(BF16)
    nt = (((1,), (1,)), ((), ()))
    qt = lax.dot_general(wuqt_ref[...], cq, nt, preferred_element_type=F32)
    for hh in range(MLA_HEADS):
        blk = qt[hh * LANES:(hh + 1) * LANES]
        x1 = blk[MLA_NOPE:MLA_NOPE + half]
        x2 = blk[MLA_NOPE + half:MLA_NOPE + MLA_ROPE]
        roped = jnp.concatenate(
            [blk[:MLA_NOPE], x1 * c - x2 * s, x2 * c + x1 * s, blk[MLA_NOPE + MLA_ROPE:]], axis=0)
        q_ref[0, hh] = (roped * scale).astype(BF16)
    v_ref[0, 0] = lax.dot_general(wuvt_ref[...], ckv, nt, preferred_element_type=F32).astype(BF16)


def _mla_proj(h, pos, nw, w_in, q_norm, kv_norm, w_uq, w_ukv, B, S):
    T, D = h.shape
    H = MLA_HEADS
    tm = min(TM_PROJ, S)
    ns = S // tm
    qk = MLA_NOPE + MLA_ROPE
    pad_r = LANES - qk
    w_in_p = jnp.concatenate([
        w_in[:, :MLA_Q_RANK + MLA_KV_RANK],
        jnp.zeros((D, MLA_NOPE), F32), w_in[:, MLA_Q_RANK + MLA_KV_RANK:], jnp.zeros((D, pad_r), F32)],
        axis=1).astype(BF16)
    wq = w_uq.reshape(MLA_Q_RANK, H, qk)
    wq_t = jnp.pad(wq, ((0, 0), (0, 0), (0, pad_r))).reshape(MLA_Q_RANK, H * LANES).T.astype(BF16)
    wkv = w_ukv.reshape(MLA_KV_RANK, H, MLA_NOPE + MLA_V)
    wk_p = jnp.pad(wkv[:, :, :MLA_NOPE], ((0, 0), (0, 0), (0, LANES - MLA_NOPE)))
    wk_p = wk_p.reshape(MLA_KV_RANK, H * LANES).astype(BF16)
    wv_t = wkv[:, :, MLA_NOPE:].reshape(MLA_KV_RANK, H * MLA_V).T.astype(BF16)
    half = MLA_ROPE // 2
    invf = (ROPE_BASE ** (-jnp.arange(half, dtype=F32) / half)).reshape(half, 1)
    wl = w_in_p.shape[1]
    out_shape = (
        jax.ShapeDtypeStruct((B, H, LANES, S), BF16),
        jax.ShapeDtypeStruct((B, H, S, LANES), BF16),
        jax.ShapeDtypeStruct((B, ns, H * MLA_V, tm), BF16),
    )
    scale = qk ** -0.5 * 1.4426950408889634
    return pl.pallas_call(
        functools.partial(_mla_proj_kernel, scale=scale),
        out_shape=out_shape,
        grid=(T // tm,),
        in_specs=[
            pl.BlockSpec((tm, D), lambda i: (i, 0)),
            pl.BlockSpec((1, tm), lambda i: (0, i)),
            _const_spec((1, D)),
            _const_spec((D, wl)),
            _const_spec((1, MLA_Q_RANK)),
            _const_spec((1, MLA_KV_RANK)),
            _const_spec((H * LANES, MLA_Q_RANK)),
            _const_spec((MLA_KV_RANK, H * LANES)),
            _const_spec((H * MLA_V, MLA_KV_RANK)),
            _const_spec((half, 1)),
        ],
        out_specs=(pl.BlockSpec((1, H, LANES, tm), lambda i: (i // ns, 0, 0, i % ns)),
                   pl.BlockSpec((1, H, tm, LANES), lambda i: (i // ns, 0, i % ns, 0)),
                   pl.BlockSpec((1, 1, H * MLA_V, tm), lambda i: (i // ns, i % ns, 0, 0))),
        compiler_params=_params("parallel"),
        name="mla_proj",
    )(h, pos, nw.reshape(1, D), w_in_p, q_norm.reshape(1, -1), kv_norm.reshape(1, -1), wq_t, wk_p, wv_t,
      invf)


def _attn_kernel(q_ref, k_ref, v_ref, o_ref, m_scr, acc_scr, s_scr, *, tq):
    qi = pl.program_id(2)
    nh = q_ref.shape[1]
    m_scr[...] = jnp.full(m_scr.shape, -jnp.inf, F32)
    acc_scr[...] = jnp.zeros(acc_scr.shape, F32)

    def scores(j, hh):
        return _dot(k_ref[0, hh, pl.ds(pl.multiple_of(j * tq, tq), tq), :], q_ref[0, hh])

    ones = jnp.ones((SUM_ROWS, tq), BF16)

    def softmax_pv(s, j, hh, masked):
        if masked:
            kv = lax.broadcasted_iota(I32, s.shape, 0)
            qq = lax.broadcasted_iota(I32, s.shape, 1)
            s = jnp.where(kv <= qq, s, -jnp.inf)
        m_prev = m_scr[hh]
        m_new = jnp.maximum(m_prev, jnp.max(s, axis=0, keepdims=True))
        alpha = jnp.exp2(m_prev - m_new)
        p = jnp.exp2(s - m_new).astype(BF16)
        vt = jnp.concatenate([v_ref[0, j, hh * MLA_V:(hh + 1) * MLA_V, :], ones], axis=0)
        acc_scr[hh] = alpha * acc_scr[hh] + _dot(vt, p)
        m_scr[hh] = m_new

    def stage(j, slot, masked, prefetch):
        for hh in range(nh):
            s = s_scr[slot, hh]
            if prefetch:
                s_scr[1 - slot, hh] = scores(j + 1, hh)
            softmax_pv(s, j, hh, masked)

    for hh in range(nh):
        s_scr[0, hh] = scores(0, hh)

    def pair(i, carry):
        stage(2 * i, 0, False, True)
        stage(2 * i + 1, 1, False, True)
        return carry

    lax.fori_loop(0, qi // 2, pair, 0)

    @pl.when(qi % 2 == 1)
    def _():
        stage(qi - 1, 0, False, True)
        stage(qi, 1, True, False)

    @pl.when(qi % 2 == 0)
    def _():
        stage(qi, 0, True, False)
    out = jnp.concatenate([acc_scr[hh, :MLA_V] / acc_scr[hh, MLA_V:MLA_V + 1] for hh in range(nh)], axis=0)
    o_ref[0] = out.T.astype(BF16)


def _attention(qt, k, vt):
    B, H, S, _ = k.shape
    tq = vt.shape[3]
    nh = HEADS_PER_STEP
    return pl.pallas_call(
        functools.partial(_attn_kernel, tq=tq),
        out_shape=jax.ShapeDtypeStruct((B, S, H * MLA_V), BF16),
        grid=(B, H // nh, S // tq),
        in_specs=[
            pl.BlockSpec((1, nh, LANES, tq), lambda b, hp, qi: (b, hp, 0, qi)),
            pl.BlockSpec((1, nh, S, LANES), lambda b, hp, qi: (b, hp, 0, 0)),
            pl.BlockSpec((1, S // tq, nh * MLA_V, tq), lambda b, hp, qi: (b, 0, hp, 0)),
        ],
        out_specs=pl.BlockSpec((1, tq, nh * MLA_V), lambda b, hp, qi: (b, qi, hp)),
        scratch_shapes=[pltpu.VMEM((nh, 1, tq), F32), pltpu.VMEM((nh, MLA_V + SUM_ROWS, tq), F32),
                        pltpu.VMEM((2, nh, tq, tq), F32)],
        compiler_params=_params("parallel", "parallel", "arbitrary"),
        name="mla_attention",
    )(qt, k, vt)


def _matmul_res_kernel(x_ref, w_ref, h_ref, o_ref):
    o_ref[...] = h_ref[...] + _dot(x_ref[...], w_ref[...])


def _matmul_residual(x, w, h):
    T, K = x.shape
    N = w.shape[1]
    tm = min(TM_RES, T)
    return pl.pallas_call(
        _matmul_res_kernel,
        out_shape=jax.ShapeDtypeStruct((T, N), F32),
        grid=(T // tm,),
        in_specs=[pl.BlockSpec((tm, K), lambda i: (i, 0)), _const_spec((K, N)),
                  pl.BlockSpec((tm, N), lambda i: (i, 0))],
        out_specs=pl.BlockSpec((tm, N), lambda i: (i, 0)),
        compiler_params=_params("parallel"),
        name="matmul_residual",
    )(x, w.astype(BF16), h)


def _swiglu(xn, wg, wu, wd):
    g = _dot(xn, wg)
    u = _dot(xn, wu)
    hmid = (g * jax.nn.sigmoid(g) * u).astype(BF16)
    return _dot(hmid, wd)


def _swiglu_steps(load_x, nw_ref, wg_ref, wu_ref, wd_ref, xn_scr, acc_scr, rows=None):
    j = pl.program_id(1)
    tm = xn_scr.shape[0]

    @pl.when(j == 0)
    def _():
        xn_scr[...] = _rmsnorm(load_x(), nw_ref[...]).astype(BF16)
        acc_scr[...] = jnp.zeros(acc_scr.shape, F32)

    def weights():
        return wg_ref[...].astype(BF16), wu_ref[...].astype(BF16), wd_ref[...].astype(BF16)

    def whole():
        acc_scr[...] += _swiglu(xn_scr[...], *weights())

    if rows is None:
        whole()
        return
    pl.when(rows > tm - ROW_CHUNK)(whole)

    @pl.when(rows <= tm - ROW_CHUNK)
    def _():
        w = weights()

        def chunk(c, carry):
            r = pl.ds(pl.multiple_of(c * ROW_CHUNK, ROW_CHUNK), ROW_CHUNK)
            acc_scr[r, :] += _swiglu(xn_scr[r, :], *w)
            return carry

        lax.fori_loop(0, (rows + ROW_CHUNK - 1) // ROW_CHUNK, chunk, 0)


def _ffn_dense_kernel(x_ref, nw_ref, wg_ref, wu_ref, wd_ref, o_ref, xn_scr, acc_scr):
    _swiglu_steps(lambda: x_ref[...], nw_ref, wg_ref, wu_ref, wd_ref, xn_scr, acc_scr)

    @pl.when(pl.program_id(1) == pl.num_programs(1) - 1)
    def _():
        o_ref[...] = x_ref[...] + acc_scr[...]


def _ffn_dense(h, nw, wg, wu, wd, layer):
    T, D = h.shape
    FF = wg.shape[2]
    tm = min(TM_FFN, T)
    tf = min(TF_FFN, FF)
    return pl.pallas_call(
        _ffn_dense_kernel,
        out_shape=jax.ShapeDtypeStruct((T, D), F32),
        grid=(T // tm, FF // tf),
        in_specs=[
            pl.BlockSpec((tm, D), lambda i, j: (i, 0)),
            _const_spec((1, D)),
            pl.BlockSpec((None, D, tf), lambda i, j: (layer, 0, j)),
            pl.BlockSpec((None, D, tf), lambda i, j: (layer, 0, j)),
            pl.BlockSpec((None, tf, D), lambda i, j: (layer, j, 0)),
        ],
        out_specs=pl.BlockSpec((tm, D), lambda i, j: (i, 0)),
        scratch_shapes=[pltpu.VMEM((tm, D), BF16), pltpu.VMEM((tm, D), F32)],
        compiler_params=_params("parallel", "arbitrary"),
        name="ffn_dense",
    )(h, nw.reshape(1, D), wg, wu, wd)


def _rows_to_tile(ref, tm):
    return jnp.concatenate([ref[pl.ds(s, tm, stride=SUBLANES), :] for s in range(SUBLANES)], axis=1)


def _tile_to_rows(ref, x, tm):
    for s in range(SUBLANES):
        ref[pl.ds(s, tm, stride=SUBLANES), :] = x[:, s * LANES:(s + 1) * LANES]


def _ffn_moe_kernel(te_ref, nv_ref, tr_ref, x_ref, nw_ref, wg_ref, wu_ref, wd_ref, o_ref, xn_scr, acc_scr):
    del te_ref
    tm = xn_scr.shape[0]
    valid = pl.program_id(0) < nv_ref[0]
    last = pl.program_id(1) == pl.num_programs(1) - 1

    @pl.when(valid)
    def _():
        _swiglu_steps(lambda: _rows_to_tile(x_ref, tm), nw_ref, wg_ref, wu_ref, wd_ref, xn_scr, acc_scr,
                      rows=tr_ref[pl.program_id(0)])

        @pl.when(last)
        def _():
            _tile_to_rows(o_ref, acc_scr[...], tm)

    @pl.when(jnp.logical_and(jnp.logical_not(valid), last))
    def _():
        o_ref[...] = jnp.zeros(o_ref.shape, F32)


def _ffn_moe(xs, nw, wg, wu, wd, layer, tile_expert, n_valid, tile_rows, tm):
    D = wg.shape[2]
    FF = wg.shape[3]
    P = xs.shape[0] // SUBLANES
    tf = min(TF_FFN, FF)
    nj = FF // tf

    def tile(i, nv):
        return jnp.maximum(jnp.minimum(i, nv[0] - 1), 0)

    def col(i, j, nv):
        return jnp.where(i < nv[0], j, nj - 1)

    def up_map(i, j, te, nv, tr):
        return (layer, te[tile(i, nv)], 0, col(i, j, nv))

    def down_map(i, j, te, nv, tr):
        return (layer, te[tile(i, nv)], col(i, j, nv), 0)

    grid_spec = pltpu.PrefetchScalarGridSpec(
        num_scalar_prefetch=3,
        grid=(P // tm, nj),
        in_specs=[
            pl.BlockSpec((tm * SUBLANES, LANES), lambda i, j, te, nv, tr: (tile(i, nv), 0)),
            pl.BlockSpec((1, D), lambda i, j, te, nv, tr: (0, 0)),
            pl.BlockSpec((None, None, D, tf), up_map),
            pl.BlockSpec((None, None, D, tf), up_map),
            pl.BlockSpec((None, None, tf, D), down_map),
        ],
        out_specs=pl.BlockSpec((tm * SUBLANES, LANES), lambda i, j, te, nv, tr: (i, 0)),
        scratch_shapes=[pltpu.VMEM((tm, D), BF16), pltpu.VMEM((tm, D), F32)],
    )
    return pl.pallas_call(
        _ffn_moe_kernel,
        out_shape=jax.ShapeDtypeStruct((P * SUBLANES, LANES), F32),
        grid_spec=grid_spec,
        compiler_params=_params("arbitrary", "arbitrary"),
        name="ffn_moe",
    )(tile_expert, n_valid, tile_rows, xs, nw.reshape(1, D), wg, wu, wd)


def _lru_kernel(h_ref, nw_ref, win_ref, cw_ref, cb_ref, wax_ref, ba_ref, bx_ref, lam_ref, wout_ref,
                o_ref, halo_scr, carry_scr, a_scr, b_scr, hs_scr):
    tt, W = h_ref.shape[1], a_scr.shape[2]
    G = tt // SUBLANES

    @pl.when(pl.program_id(1) == 0)
    def _():
        halo_scr[...] = jnp.zeros(halo_scr.shape, F32)
        carry_scr[...] = jnp.zeros(carry_scr.shape, F32)

    x = h_ref[0]
    xn = _rmsnorm(x, nw_ref[...]).astype(BF16)
    z = _dot(xn, win_ref[...])
    gate = jax.nn.gelu(z[:, :W])
    xb = z[:, W:]
    xe = jnp.concatenate([halo_scr[...], xb], axis=0)
    halo_scr[...] = xb[tt - SUBLANES:, :]
    cw = cw_ref[...]
    xc = cb_ref[...]
    for kk in range(CONV_WIDTH):
        off = SUBLANES - (CONV_WIDTH - 1) + kk
        xc = xc + cw[kk:kk + 1, :] * xe[off:off + tt, :]
    xcb = xc.astype(BF16)
    hd = W // LRU_HEADS
    r_parts, i_parts = [], []
    for n in range(LRU_HEADS):
        ri = _dot(xcb[:, n * hd:(n + 1) * hd], wax_ref[n])
        r_parts.append(ri[:, :hd])
        i_parts.append(ri[:, hd:])
    r = jax.nn.sigmoid(jnp.concatenate(r_parts, axis=1) + ba_ref[...])
    ig = jax.nn.sigmoid(jnp.concatenate(i_parts, axis=1) + bx_ref[...])
    nl = -lam_ref[...]
    softplus = jnp.maximum(nl, 0.0) + jnp.log1p(jnp.exp(-jnp.abs(nl)))
    a = jnp.exp(-LRU_C * r * softplus)
    b = jnp.sqrt(1.0 - a * a) * (ig * xc)

    A = a.reshape(G, SUBLANES, W)
    Bv = b.reshape(G, SUBLANES, W)
    sub = lax.broadcasted_iota(I32, A.shape, 1)
    for d in (1, 2, 4):
        keep = sub >= d
        a_sh = jnp.where(keep, pltpu.roll(A, d, 1), 1.0)
        b_sh = jnp.where(keep, pltpu.roll(Bv, d, 1), 0.0)
        Bv = A * b_sh + Bv
        A = A * a_sh
    a_scr[...] = A
    b_scr[...] = Bv

    def body(g, hprev):
        hg = b_scr[g] + a_scr[g] * hprev
        hs_scr[g] = hg
        return hg[SUBLANES - 1:, :]

    carry_scr[...] = lax.fori_loop(0, G, body, carry_scr[...], unroll=8)
    y = (hs_scr[...].reshape(tt, W) * gate).astype(BF16)
    o_ref[0] = x + _dot(y, wout_ref[...])


def _lru_mixer(h, nw, w_in, conv_w, conv_b, w_a, b_a, w_x, b_x, lam, w_out):
    B, S, D = h.shape
    W = w_out.shape[0]
    tt = min(TT_LRU, S)
    hd = W // LRU_HEADS
    wax = jnp.concatenate([w_a, w_x], axis=2).astype(BF16)
    G = tt // SUBLANES
    return pl.pallas_call(
        _lru_kernel,
        out_shape=jax.ShapeDtypeStruct((B, S, D), F32),
        grid=(B, S // tt),
        in_specs=[
            pl.BlockSpec((1, tt, D), lambda b, t: (b, t, 0)),
            _const_spec((1, D)),
            _const_spec((D, 2 * W)),
            _const_spec((CONV_WIDTH, W)),
            _const_spec((1, W)),
            _const_spec((LRU_HEADS, hd, 2 * hd)),
            _const_spec((1, W)),
            _const_spec((1, W)),
            _const_spec((1, W)),
            _const_spec((W, D)),
        ],
        out_specs=pl.BlockSpec((1, tt, D), lambda b, t: (b, t, 0)),
        scratch_shapes=[
            pltpu.VMEM((SUBLANES, W), F32), pltpu.VMEM((1, W), F32),
            pltpu.VMEM((G, SUBLANES, W), F32), pltpu.VMEM((G, SUBLANES, W), F32),
            pltpu.VMEM((G, SUBLANES, W), F32),
        ],
        compiler_params=_params("parallel", "arbitrary"),
        name="rglru_mixer",
    )(h, nw.reshape(1, D), w_in.astype(BF16), conv_w, conv_b.reshape(1, W), wax, b_a.reshape(1, W),
      b_x.reshape(1, W), lam.reshape(1, W), w_out.astype(BF16))


def _pool_kernel(h_ref, halo_ref, nw_ref, wg_ref, sc_ref, o_ref, *, blocks_per_seq):
    tm, D = h_ref.shape
    gsz = D // len(POOL_WINDOWS)
    x = h_ref[...]
    first = pl.program_id(0) % blocks_per_seq == 0
    xn = _rmsnorm(x, nw_ref[...])
    hn_halo = _rmsnorm(halo_ref[...], nw_ref[...])
    hn_halo = jnp.where(first, 0.0, hn_halo)
    e = jnp.concatenate([hn_halo, xn], axis=0)
    t = (pl.program_id(0) % blocks_per_seq) * tm + lax.broadcasted_iota(I32, (tm, 1), 0)
    outs = []
    for g, w in enumerate(POOL_WINDOWS):
        cur = e[:, g * gsz:(g + 1) * gsz]
        base = 0
        d = 1
        while d < w:
            cur = cur[d:, :] + cur[:-d, :]
            base += d
            d *= 2
        win = cur[POOL_HALO - base:POOL_HALO - base + tm, :]
        count = jnp.minimum(t + 1, w).astype(F32)
        pooled = (win / count - xn[:, g * gsz:(g + 1) * gsz]).astype(BF16)
        outs.append(_dot(pooled, wg_ref[g]))
    o_ref[...] = x + jnp.concatenate(outs, axis=1) * sc_ref[...]


def _pool_mixer(h, nw, w_grp, scale, S):
    T, D = h.shape
    tm = min(TM_POOL, S)
    bps = S // tm
    hb = tm // POOL_HALO
    ng, gsz = w_grp.shape[0], w_grp.shape[1]
    return pl.pallas_call(
        functools.partial(_pool_kernel, blocks_per_seq=bps),
        out_shape=jax.ShapeDtypeStruct((T, D), F32),
        grid=(T // tm,),
        in_specs=[
            pl.BlockSpec((tm, D), lambda i: (i, 0)),
            pl.BlockSpec((POOL_HALO, D), lambda i: (jnp.maximum(i * hb - 1, 0), 0)),
            _const_spec((1, D)),
            _const_spec((ng, gsz, gsz)),
            _const_spec((1, D)),
        ],
        out_specs=pl.BlockSpec((tm, D), lambda i: (i, 0)),
        compiler_params=_params("parallel"),
        name="pool_mixer",
    )(h, h, nw.reshape(1, D), w_grp.astype(BF16), scale.reshape(1, D))


def _gmlp_kernel(h_ref, nw_ref, win_ref, lg_ref, lb_ref, ws_ref, bs_ref, wout_ref, o_ref):
    tm, D = h_ref.shape
    W = wout_ref.shape[0]
    gw = W // GMLP_GROUPS
    x = h_ref[...]
    xn = _rmsnorm(x, nw_ref[...]).astype(BF16)
    z = jax.nn.gelu(_dot(xn, win_ref[...]))
    u = z[:, :W]
    v = z[:, W:]
    mu = jnp.mean(v, axis=-1, keepdims=True)
    vc = v - mu
    var = jnp.mean(vc * vc, axis=-1, keepdims=True)
    vb = (vc * lax.rsqrt(var + LN_EPS) * lg_ref[...] + lb_ref[...]).astype(BF16)
    row = lax.broadcasted_iota(I32, (GMLP_CHUNK, GMLP_CHUNK), 0)
    col = lax.broadcasted_iota(I32, (GMLP_CHUNK, GMLP_CHUNK), 1)
    bs = bs_ref[...]
    ws = [jnp.where(col <= row, ws_ref[g], 0.0).astype(BF16) for g in range(GMLP_GROUPS)]
    rows = []
    for c in range(tm // GMLP_CHUNK):
        cols = []
        for g in range(GMLP_GROUPS):
            blk = vb[c * GMLP_CHUNK:(c + 1) * GMLP_CHUNK, g * gw:(g + 1) * gw]
            cols.append(_dot(ws[g], blk) + bs[:, g:g + 1])
        rows.append(jnp.concatenate(cols, axis=1))
    v2 = jnp.concatenate(rows, axis=0)
    o_ref[...] = x + _dot((u * v2).astype(BF16), wout_ref[...])


def _gmlp_mixer(h, nw, w_in, ln_g, ln_b, w_s, b_s, w_out):
    T, D = h.shape
    W = w_out.shape[0]
    tm = min(TM_GMLP, T)
    return pl.pallas_call(
        _gmlp_kernel,
        out_shape=jax.ShapeDtypeStruct((T, D), F32),
        grid=(T // tm,),
        in_specs=[
            pl.BlockSpec((tm, D), lambda i: (i, 0)),
            _const_spec((1, D)),
            _const_spec((D, 2 * W)),
            _const_spec((1, W)),
            _const_spec((1, W)),
            _const_spec((GMLP_GROUPS, GMLP_CHUNK, GMLP_CHUNK)),
            _const_spec((GMLP_CHUNK, GMLP_GROUPS)),
            _const_spec((W, D)),
        ],
        out_specs=pl.BlockSpec((tm, D), lambda i: (i, 0)),
        compiler_params=_params("parallel"),
        name="gmlp_mixer",
    )(h, nw.reshape(1, D), w_in.astype(BF16), ln_g.reshape(1, W), ln_b.reshape(1, W), w_s, b_s.T,
      w_out.astype(BF16))


def _router_kernel(h_ref, nw_ref, whi_ref, wlo_ref, meta_ref, gate_ref, cnt_ref, cnt_scr):
    tm = h_ref.shape[0]

    @pl.when(pl.program_id(0) == 0)
    def _():
        cnt_scr[...] = jnp.zeros(cnt_scr.shape, F32)

    xn = _rmsnorm(h_ref[...], nw_ref[...])
    x_hi = xn.astype(BF16)
    x_lo = (xn - x_hi.astype(F32)).astype(BF16)
    nt = (((1,), (1,)), ((), ()))
    logits = (lax.dot_general(whi_ref[...], x_hi, nt, preferred_element_type=F32)
              + lax.dot_general(whi_ref[...], x_lo, nt, preferred_element_type=F32)
              + lax.dot_general(wlo_ref[...], x_hi, nt, preferred_element_type=F32))
    eid = lax.broadcasted_iota(I32, logits.shape, 0).astype(F32)
    m1 = jnp.max(logits, axis=0, keepdims=True)
    e1 = jnp.min(jnp.where(logits == m1, eid, float(N_EXPERTS)), axis=0, keepdims=True)
    rest = jnp.where(eid == e1, -jnp.inf, logits)
    m2 = jnp.max(rest, axis=0, keepdims=True)
    e2 = jnp.min(jnp.where(rest == m2, eid, float(N_EXPERTS)), axis=0, keepdims=True)
    ex = jnp.exp(m2 - m1)
    g1 = 1.0 / (1.0 + ex)
    g2 = ex / (1.0 + ex)
    hit1 = eid == e1
    hit2 = eid == e2
    onehot = jnp.where(hit1 | hit2, 1.0, 0.0)
    row = lax.broadcasted_iota(I32, (tm, tm), 0)
    col = lax.broadcasted_iota(I32, (tm, tm), 1)
    earlier = jnp.where(row < col, 1.0, 0.0).astype(BF16)
    before = _dot(onehot.astype(BF16), earlier) + cnt_scr[...]
    r1 = jnp.sum(jnp.where(hit1, before, 0.0), axis=0, keepdims=True)
    r2 = jnp.sum(jnp.where(hit2, before, 0.0), axis=0, keepdims=True)
    cnt_scr[...] += jnp.sum(onehot, axis=1, keepdims=True)
    zero = jnp.zeros((SUBLANES - 4, tm), F32)
    meta_ref[...] = jnp.concatenate([e1, e2, r1, r2, zero], axis=0).astype(I32)
    gate_ref[...] = jnp.concatenate([g1, g2, zero, 0.0 * g1, 0.0 * g2], axis=0)
    cnt_ref[...] = jnp.broadcast_to(cnt_scr[...], cnt_ref.shape)


def _router(h, nw, w_router):
    T, D = h.shape
    assert N_EXPERTS == SUBLANES
    tm = min(TM_ROUTE, T)
    tok_spec = pl.BlockSpec((SUBLANES, tm), lambda i: (0, i))
    w_hi = w_router.T.astype(BF16)
    w_lo = (w_router.T - w_hi.astype(F32)).astype(BF16)
    return pl.pallas_call(
        _router_kernel,
        out_shape=(jax.ShapeDtypeStruct((SUBLANES, T), I32), jax.ShapeDtypeStruct((SUBLANES, T), F32),
                   jax.ShapeDtypeStruct((N_EXPERTS, LANES), F32)),
        grid=(T // tm,),
        in_specs=[pl.BlockSpec((tm, D), lambda i: (i, 0)), _const_spec((1, D)), _const_spec((N_EXPERTS, D)),
                  _const_spec((N_EXPERTS, D))],
        out_specs=(tok_spec, tok_spec, _const_spec((N_EXPERTS, LANES))),
        scratch_shapes=[pltpu.VMEM((N_EXPERTS, 1), F32)],
        compiler_params=_params("arbitrary"),
        name="moe_router",
    )(h, nw.reshape(1, D), w_hi, w_lo)


ZERO_ROWS = 512


def _dispatch_kernel(zs_ref, zl_ref, d1_ref, d2_ref, h_ref, xs_ref, rows_scr, zero_scr, sem, zsem):
    tm = h_ref.shape[0]
    _tile_to_rows(rows_scr, h_ref[...], tm)

    def copies(r):
        src = rows_scr.at[pl.ds(pl.multiple_of(r * SUBLANES, SUBLANES), SUBLANES), :]
        return [pltpu.make_async_copy(
            src, xs_ref.at[pl.ds(pl.multiple_of(d[0, 0, r] * SUBLANES, SUBLANES), SUBLANES), :], sem)
            for d in (d1_ref, d2_ref)]

    def start(r, c):
        for prio, cp in enumerate(copies(r)):
            cp.start(priority=prio)
        return c

    def wait(r, c):
        for cp in copies(r):
            cp.wait()
        return c

    def zero_copies(fn):
        def zero_dma(off, size, pred):
            cp = pltpu.make_async_copy(
                zero_scr.at[pl.ds(0, size * SUBLANES), :],
                xs_ref.at[pl.ds(pl.multiple_of(off * SUBLANES, SUBLANES), size * SUBLANES), :], zsem)
            pl.when(pred)(functools.partial(fn, cp))

        for k in range(N_EXPERTS + 1):
            tail = k == N_EXPERTS
            max_rows = 2 * ZERO_ROWS * (N_EXPERTS if tail else 1)
            n = zl_ref[k]
            whole = n // ZERO_ROWS
            for c in range(max_rows // ZERO_ROWS):
                zero_dma(zs_ref[k] + c * ZERO_ROWS, ZERO_ROWS, c < whole)
            base = zs_ref[k] + whole * ZERO_ROWS
            rest = n % ZERO_ROWS
            size = ZERO_ROWS // 2
            while size >= 1:
                zero_dma(base + (rest // (2 * size)) * (2 * size), size, (rest // size) % 2 == 1)
                size //= 2

    first = pl.program_id(0) == 0

    @pl.when(first)
    def _():
        zero_scr[...] = jnp.zeros(zero_scr.shape, F32)
        zero_copies(lambda cp: cp.start())

    lax.fori_loop(0, tm, start, 0, unroll=4)
    lax.fori_loop(0, tm, wait, 0, unroll=4)

    @pl.when(first)
    def _():
        zero_copies(lambda cp: cp.wait())


def _dispatch(h, d1, d2, zero_start, zero_len, P):
    T, D = h.shape
    tm = min(TM_DISPATCH, T)
    nb = T // tm
    smem = functools.partial(pl.BlockSpec, memory_space=pltpu.SMEM)
    grid_spec = pltpu.PrefetchScalarGridSpec(
        num_scalar_prefetch=2,
        grid=(nb,),
        in_specs=[
            smem((1, 1, tm), lambda i, zs, zl: (i, 0, 0)),
            smem((1, 1, tm), lambda i, zs, zl: (i, 0, 0)),
            pl.BlockSpec((tm, D), lambda i, zs, zl: (i, 0)),
        ],
        out_specs=pl.BlockSpec(memory_space=pl.ANY),
        scratch_shapes=[pltpu.VMEM((tm * SUBLANES, LANES), F32), pltpu.VMEM((ZERO_ROWS * SUBLANES, LANES), F32),
                        pltpu.SemaphoreType.DMA, pltpu.SemaphoreType.DMA],
    )
    return pl.pallas_call(
        _dispatch_kernel,
        out_shape=jax.ShapeDtypeStruct((P * SUBLANES, LANES), F32),
        grid_spec=grid_spec,
        compiler_params=_params("arbitrary"),
        name="moe_dispatch",
    )(zero_start, zero_len, d1.reshape(nb, 1, tm), d2.reshape(nb, 1, tm), h)


def _combine_kernel(d1_ref, d2_ref, n1_ref, n2_ref, h_ref, gate_ref, ys_ref, *rest, final_norm):
    if final_norm:
        fw_ref, o_ref, buf1, buf2, sem = rest
    else:
        o_ref, buf1, buf2, sem = rest
    tm = h_ref.shape[0]
    step = pl.program_id(0)
    slot = step % 2

    def copies(srcs, buf, r):
        dst = pl.ds(pl.multiple_of(r * SUBLANES, SUBLANES), SUBLANES)
        return [pltpu.make_async_copy(
            ys_ref.at[pl.ds(pl.multiple_of(d[0, 0, r] * SUBLANES, SUBLANES), SUBLANES), :],
            b.at[buf, dst, :], sem.at[buf]) for d, b in zip(srcs, (buf1, buf2))]

    def issue(srcs, buf):
        def start(r, c):
            for prio, cp in enumerate(copies(srcs, buf, r)):
                cp.start(priority=prio)
            return c

        lax.fori_loop(0, tm, start, 0, unroll=4)

    @pl.when(step == 0)
    def _():
        issue((d1_ref, d2_ref), slot)

    @pl.when(step + 1 < pl.num_programs(0))
    def _():
        issue((n1_ref, n2_ref), 1 - slot)

    def wait(r, c):
        for cp in copies((d1_ref, d2_ref), slot, r):
            cp.wait()
        return c

    lax.fori_loop(0, tm, wait, 0, unroll=4)
    gates = gate_ref[...]
    out = (h_ref[...] + gates[:, 0:1] * _rows_to_tile(buf1.at[slot], tm)
           + gates[:, 1:2] * _rows_to_tile(buf2.at[slot], tm))
    if final_norm:
        out = _rmsnorm(out, fw_ref[...])
    o_ref[...] = out


def _combine(h, gates, ys, d1, d2, final_w=None):
    T, D = h.shape
    tm = min(TM_COMBINE, T)
    nb = T // tm
    smem = functools.partial(pl.BlockSpec, memory_space=pltpu.SMEM)
    in_specs = [
        smem((1, 1, tm), lambda i: (i, 0, 0)),
        smem((1, 1, tm), lambda i: (i, 0, 0)),
        smem((1, 1, tm), lambda i: (jnp.minimum(i + 1, nb - 1), 0, 0)),
        smem((1, 1, tm), lambda i: (jnp.minimum(i + 1, nb - 1), 0, 0)),
        pl.BlockSpec((tm, D), lambda i: (i, 0)),
        pl.BlockSpec((tm, TOP_K), lambda i: (i, 0)),
        pl.BlockSpec(memory_space=pl.ANY),
    ]
    d1 = d1.reshape(nb, 1, tm)
    d2 = d2.reshape(nb, 1, tm)
    args = [d1, d2, d1, d2, h, gates, ys]
    if final_w is not None:
        in_specs.append(_const_spec((1, D)))
        args.append(final_w.reshape(1, D))
    buf = pltpu.VMEM((2, tm * SUBLANES, LANES), F32)
    return pl.pallas_call(
        functools.partial(_combine_kernel, final_norm=final_w is not None),
        out_shape=jax.ShapeDtypeStruct((T, D), F32),
        grid=(nb,),
        in_specs=in_specs,
        out_specs=pl.BlockSpec((tm, D), lambda i: (i, 0)),
        scratch_shapes=[buf, buf, pltpu.SemaphoreType.DMA((2,))],
        compiler_params=_params("arbitrary"),
        name="moe_combine",
    )(*args)


def _moe_layer(h, nw, w_router, wg, wu, wd, layer, final_w=None):
    T, D = h.shape
    meta, gates, counts = _router(h, nw, w_router)
    tm = min(TM_MOE, T)
    sizes = counts[:, 0].astype(I32)
    padded = (sizes + tm - 1) // tm * tm
    pad_end = jnp.cumsum(padded)
    pad_start = pad_end - padded
    d1 = pad_start[meta[0]] + meta[2]
    d2 = pad_start[meta[1]] + meta[3]
    n_tiles = (T * TOP_K + N_EXPERTS * (tm - 1)) // tm
    tile_start = jnp.arange(n_tiles, dtype=I32) * tm
    tile_expert = jnp.minimum(jnp.sum(pad_end[None, :] <= tile_start[:, None], axis=1), N_EXPERTS - 1)
    n_valid = (pad_end[-1] // tm).reshape(1)
    assert tm <= 2 * ZERO_ROWS
    total = n_tiles * tm
    zero_start = jnp.concatenate([pad_start + sizes, pad_end[-1:]])
    zero_len = jnp.concatenate([padded - sizes, total - pad_end[-1:]])
    xs = _dispatch(h, d1, d2, zero_start, zero_len, total)
    tile_rows = jnp.clip((pad_start + sizes)[tile_expert] - tile_start, 0, tm)
    ys = _ffn_moe(xs, nw, wg, wu, wd, layer, tile_expert.astype(I32), n_valid.astype(I32),
                  tile_rows.astype(I32), tm)
    return _combine(h, gates[:TOP_K].T, ys, d1, d2, final_w)


def kernel(x, positions, norm_mix, norm_ffn, norm_final, mla_w_in, mla_q_norm, mla_kv_norm, mla_w_uq,
           mla_w_ukv, mla_w_o, lru_w_in, lru_conv_w, lru_conv_b, lru_w_a, lru_b_a, lru_w_x, lru_b_x,
           lru_lam, lru_w_out, pool_w_grp, pool_scale, gmlp_w_in, gmlp_ln_g, gmlp_ln_b, gmlp_w_s,
           gmlp_b_s, gmlp_w_out, ffn_w_gate, ffn_w_up, ffn_w_down, moe_w_router, moe_w_gate, moe_w_up,
           moe_w_down):
    B, S, D = x.shape
    T = B * S
    depth = norm_mix.shape[0]
    h = x.reshape(T, D)
    for i in range(depth):
        m, j = i % 4, i // 4
        if m == 0:
            q, k, v = _mla_proj(h, positions.reshape(1, T), norm_mix[i], mla_w_in[j], mla_q_norm[j],
                                mla_kv_norm[j], mla_w_uq[j], mla_w_ukv[j], B, S)
            o = _attention(q, k, v)
            h = _matmul_residual(o.reshape(T, -1), mla_w_o[j], h)
        elif m == 1:
            h = _lru_mixer(h.reshape(B, S, D), norm_mix[i], lru_w_in[j], lru_conv_w[j], lru_conv_b[j],
                           lru_w_a[j], lru_b_a[j], lru_w_x[j], lru_b_x[j], lru_lam[j],
                           lru_w_out[j]).reshape(T, D)
        elif m == 2:
            h = _pool_mixer(h, norm_mix[i], pool_w_grp[j], pool_scale[j], S)
        else:
            h = _gmlp_mixer(h, norm_mix[i], gmlp_w_in[j], gmlp_ln_g[j], gmlp_ln_b[j], gmlp_w_s[j],
                            gmlp_b_s[j], gmlp_w_out[j])
        kk = i // 2
        if i % 2 == 0:
            h = _ffn_dense(h, norm_ffn[i], ffn_w_gate, ffn_w_up, ffn_w_down, kk)
        else:
            last = i == depth - 1
            h = _moe_layer(h, norm_ffn[i], moe_w_router[kk], moe_w_gate, moe_w_up, moe_w_down, kk,
                           norm_final if last else None)
    if depth % 2 == 1:
        raise NotImplementedError("final RMSNorm is fused into the last MoE combine")
    return h.reshape(B, S, D)
```

```python
import functools

import jax
import jax.numpy as jnp
from jax import lax
from jax.experimental import pallas as pl
from jax.experimental.pallas import tpu as pltpu

F32 = jnp.float32
BF16 = jnp.bfloat16
I32 = jnp.int32

RMS_EPS = 1e-6
LN_EPS = 1e-5

MLA_HEADS = 16
MLA_Q_RANK = 384
MLA_KV_RANK = 256
MLA_NOPE = 64
MLA_ROPE = 32
MLA_V = 64
ROPE_BASE = 10000.0
LRU_HEADS = 8
CONV_WIDTH = 4
LRU_C = 8.0
POOL_WINDOWS = (2, 4, 8, 16)
GMLP_GROUPS = 4
GMLP_CHUNK = 128
N_EXPERTS = 8
TOP_K = 2

LANES = 128
SUBLANES = 8
VMEM_LIMIT_BYTES = 56 * 1024 * 1024

TM_PROJ = 512
TQ_ATTN = 512
HEADS_PER_STEP = 8
SUM_ROWS = 16
TM_RES = 512
TM_FFN = 1024
TF_FFN = 512
ROW_CHUNK = 256
TT_LRU = 512
TM_POOL = 512
TM_GMLP = 512
TM_ROUTE = 512
TM_MOE = 1024
TM_DISPATCH = 512
TM_COMBINE = 512
POOL_HALO = 16


def _params(*sem):
    return pltpu.CompilerParams(dimension_semantics=sem, vmem_limit_bytes=VMEM_LIMIT_BYTES)


def _rmsnorm(x, g):
    return x * lax.rsqrt(jnp.mean(x * x, axis=-1, keepdims=True) + RMS_EPS) * g


def _dot(a, b):
    return jnp.dot(a, b, preferred_element_type=F32)


def _const_spec(shape):
    return pl.BlockSpec(shape, lambda *_: (0,) * len(shape))


def _mla_proj_kernel(h_ref, pos_ref, nw_ref, win_ref, qn_ref, kvn_ref, wuqt_ref, wuk_ref, wuvt_ref,
                     invf_ref, q_ref, k_ref, v_ref, *, scale):
    half = MLA_ROPE // 2
    x = h_ref[...]
    tm = x.shape[0]
    xn = _rmsnorm(x, nw_ref[...]).astype(BF16)
    lat = _dot(xn, win_ref[...])
    cq = _rmsnorm(lat[:, :MLA_Q_RANK], qn_ref[...]).astype(BF16)
    ckv = _rmsnorm(lat[:, MLA_Q_RANK:MLA_Q_RANK + MLA_KV_RANK], kvn_ref[...]).astype(BF16)
    kr = lat[:, MLA_Q_RANK + MLA_KV_RANK:]
    ang = invf_ref[...] * pos_ref[...].astype(F32)
    c = jnp.cos(ang)
    s = jnp.sin(ang)
    pad_r = LANES - MLA_NOPE - MLA_ROPE
    one = jnp.ones((MLA_NOPE, tm), F32)
    c_t = jnp.concatenate([one, c, c, jnp.ones((pad_r, tm), F32)], axis=0).T
    s_lo = jnp.concatenate([0.0 * one, -s, jnp.zeros((half + pad_r, tm), F32)], axis=0).T
    s_hi = jnp.concatenate([0.0 * one, 0.0 * s, s, jnp.zeros((pad_r, tm), F32)], axis=0).T
    kr = kr * c_t + pltpu.roll(kr, LANES - half, 1) * s_lo + pltpu.roll(kr, half, 1) * s_hi
    kn = _dot(ckv, wuk_ref[...])
    for hh in range(MLA_HEADS):
        k_ref[0, hh] = (kn[:, hh * LANES:(hh + 1) * LANES] + kr).astype(BF16)
    nt = (((1,), (1,)), ((), ()))
    qt = lax.dot_general(wuqt_ref[...], cq, nt, preferred_element_type=F32)
    for hh in range(MLA_HEADS):
        blk = qt[hh * LANES:(hh + 1) * LANES]
        x1 = blk[MLA_NOPE:MLA_NOPE + half]
        x2 = blk[MLA_NOPE + half:MLA_NOPE + MLA_ROPE]
        roped = jnp.concatenate(
            [blk[:MLA_NOPE], x1 * c - x2 * s, x2 * c + x1 * s, blk[MLA_NOPE + MLA_ROPE:]], axis=0)
        q_ref[0, hh] = (roped * scale).astype(BF16)
    v_ref[0, 0] = lax.dot_general(wuvt_ref[...], ckv, nt, preferred_element_type=F32).astype(BF16)


def _mla_proj(h, pos, nw, w_in, q_norm, kv_norm, w_uq, w_ukv, B, S):
    T, D = h.shape
    H = MLA_HEADS
    tm = min(TM_PROJ, S)
    ns = S // tm
    qk = MLA_NOPE + MLA_ROPE
    pad_r = LANES - qk
    w_in_p = jnp.concatenate([
        w_in[:, :MLA_Q_RANK + MLA_KV_RANK],
        jnp.zeros((D, MLA_NOPE), F32), w_in[:, MLA_Q_RANK + MLA_KV_RANK:], jnp.zeros((D, pad_r), F32)],
        axis=1).astype(BF16)
    wq = w_uq.reshape(MLA_Q_RANK, H, qk)
    wq_t = jnp.pad(wq, ((0, 0), (0, 0), (0, pad_r))).reshape(MLA_Q_RANK, H * LANES).T.astype(BF16)
    wkv = w_ukv.reshape(MLA_KV_RANK, H, MLA_NOPE + MLA_V)
    wk_p = jnp.pad(wkv[:, :, :MLA_NOPE], ((0, 0), (0, 0), (0, LANES - MLA_NOPE)))
    wk_p = wk_p.reshape(MLA_KV_RANK, H * LANES).astype(BF16)
    wv_t = wkv[:, :, MLA_NOPE:].reshape(MLA_KV_RANK, H * MLA_V).T.astype(BF16)
    half = MLA_ROPE // 2
    invf = (ROPE_BASE ** (-jnp.arange(half, dtype=F32) / half)).reshape(half, 1)
    wl = w_in_p.shape[1]
    out_shape = (
        jax.ShapeDtypeStruct((B, H, LANES, S), BF16),
        jax.ShapeDtypeStruct((B, H, S, LANES), BF16),
        jax.ShapeDtypeStruct((B, ns, H * MLA_V, tm), BF16),
    )
    scale = qk ** -0.5 * 1.4426950408889634
    return pl.pallas_call(
        functools.partial(_mla_proj_kernel, scale=scale),
        out_shape=out_shape,
        grid=(T // tm,),
        in_specs=[
            pl.BlockSpec((tm, D), lambda i: (i, 0)),
            pl.BlockSpec((1, tm), lambda i: (0, i)),
            _const_spec((1, D)),
            _const_spec((D, wl)),
            _const_spec((1, MLA_Q_RANK)),
            _const_spec((1, MLA_KV_RANK)),
            _const_spec((H * LANES, MLA_Q_RANK)),
            _const_spec((MLA_KV_RANK, H * LANES)),
            _const_spec((H * MLA_V, MLA_KV_RANK)),
            _const_spec((half, 1)),
        ],
        out_specs=(pl.BlockSpec((1, H, LANES, tm), lambda i: (i // ns, 0, 0, i % ns)),
                   pl.BlockSpec((1, H, tm, LANES), lambda i: (i // ns, 0, i % ns, 0)),
                   pl.BlockSpec((1, 1, H * MLA_V, tm), lambda i: (i // ns, i % ns, 0, 0))),
        compiler_params=_params("parallel"),
        name="mla_proj",
    )(h, pos, nw.reshape(1, D), w_in_p, q_norm.reshape(1, -1), kv_norm.reshape(1, -1), wq_t, wk_p, wv_t,
      invf)


def _attn_kernel(q_ref, k_ref, v_ref, o_ref, m_scr, acc_scr, s_scr, *, tq):
    qi = pl.program_id(2)
    nh = q_ref.shape[1]
    m_scr[...] = jnp.full(m_scr.shape, -jnp.inf, F32)
    acc_scr[...] = jnp.zeros(acc_scr.shape, F32)

    def scores(j, hh):
        return _dot(k_ref[0, hh, pl.ds(pl.multiple_of(j * tq, tq), tq), :], q_ref[0, hh])

    ones = jnp.ones((SUM_ROWS, tq), BF16)

    def softmax_pv(s, j, hh, masked):
        if masked:
            kv = lax.broadcasted_iota(I32, s.shape, 0)
            qq = lax.broadcasted_iota(I32, s.shape, 1)
            s = jnp.where(kv <= qq, s, -jnp.inf)
        m_prev = m_scr[hh]
        m_new = jnp.maximum(m_prev, jnp.max(s, axis=0, keepdims=True))
        alpha = jnp.exp2(m_prev - m_new)
        p = jnp.exp2(s - m_new).astype(BF16)
        vt = jnp.concatenate([v_ref[0, j, hh * MLA_V:(hh + 1) * MLA_V, :], ones], axis=0)
        acc_scr[hh] = alpha * acc_scr[hh] + _dot(vt, p)
        m_scr[hh] = m_new

    def stage(j, slot, masked, prefetch):
        for hh in range(nh):
            s = s_scr[slot, hh]
            if prefetch:
                s_scr[1 - slot, hh] = scores(j + 1, hh)
            softmax_pv(s, j, hh, masked)

    for hh in range(nh):
        s_scr[0, hh] = scores(0, hh)

    def pair(i, carry):
        stage(2 * i, 0, False, True)
        stage(2 * i + 1, 1, False, True)
        return carry

    lax.fori_loop(0, qi // 2, pair, 0)

    @pl.when(qi % 2 == 1)
    def _():
        stage(qi - 1, 0, False, True)
        stage(qi, 1, True, False)

    @pl.when(qi % 2 == 0)
    def _():
        stage(qi, 0, True, False)
    out = jnp.concatenate([acc_scr[hh, :MLA_V] / acc_scr[hh, MLA_V:MLA_V + 1] for hh in range(nh)], axis=0)
    o_ref[0] = out.T.astype(BF16)


def _attention(qt, k, vt):
    B, H, S, _ = k.shape
    tq = vt.shape[3]
    nh = HEADS_PER_STEP
    return pl.pallas_call(
        functools.partial(_attn_kernel, tq=tq),
        out_shape=jax.ShapeDtypeStruct((B, S, H * MLA_V), BF16),
        grid=(B, H // nh, S // tq),
        in_specs=[
            pl.BlockSpec((1, nh, LANES, tq), lambda b, hp, qi: (b, hp, 0, qi)),
            pl.BlockSpec((1, nh, S, LANES), lambda b, hp, qi: (b, hp, 0, 0)),
            pl.BlockSpec((1, S // tq, nh * MLA_V, tq), lambda b, hp, qi: (b, 0, hp, 0)),
        ],
        out_specs=pl.BlockSpec((1, tq, nh * MLA_V), lambda b, hp, qi: (b, qi, hp)),
        scratch_shapes=[pltpu.VMEM((nh, 1, tq), F32), pltpu.VMEM((nh, MLA_V + SUM_ROWS, tq), F32),
                        pltpu.VMEM((2, nh, tq, tq), F32)],
        compiler_params=_params("parallel", "parallel", "arbitrary"),
        name="mla_attention",
    )(qt, k, vt)


def _matmul_res_kernel(x_ref, w_ref, h_ref, o_ref):
    o_ref[...] = h_ref[...] + _dot(x_ref[...], w_ref[...])


def _matmul_residual(x, w, h):
    T, K = x.shape
    N = w.shape[1]
    tm = min(TM_RES, T)
    return pl.pallas_call(
        _matmul_res_kernel,
        out_shape=jax.ShapeDtypeStruct((T, N), F32),
        grid=(T // tm,),
        in_specs=[pl.BlockSpec((tm, K), lambda i: (i, 0)), _const_spec((K, N)),
                  pl.BlockSpec((tm, N), lambda i: (i, 0))],
        out_specs=pl.BlockSpec((tm, N), lambda i: (i, 0)),
        compiler_params=_params("parallel"),
        name="matmul_residual",
    )(x, w.astype(BF16), h)


def _swiglu(xn, wg, wu, wd, parts=1):
    width = wg.shape[1] // parts
    out = None
    hm = []
    for c in range(parts):
        cols = slice(c * width, (c + 1) * width)
        g = _dot(xn, wg[:, cols])
        u = _dot(xn, wu[:, cols])
        hm.append((g * jax.nn.sigmoid(g) * u).astype(BF16))
    for c in range(parts):
        y = _dot(hm[c], wd[c * width:(c + 1) * width])
        out = y if out is None else out + y
    return out


def _swiglu_steps(load_x, nw_ref, wg_ref, wu_ref, wd_ref, xn_scr, acc_scr, rows=None):
    j = pl.program_id(1)
    tm = xn_scr.shape[0]

    @pl.when(j == 0)
    def _():
        xn_scr[...] = _rmsnorm(load_x(), nw_ref[...]).astype(BF16)
        acc_scr[...] = jnp.zeros(acc_scr.shape, F32)

    def weights():
        return wg_ref[...].astype(BF16), wu_ref[...].astype(BF16), wd_ref[...].astype(BF16)

    def whole():
        acc_scr[...] += _swiglu(xn_scr[...], *weights(), parts=2)

    if rows is None:
        whole()
        return
    pl.when(rows > tm - ROW_CHUNK)(whole)

    @pl.when(rows <= tm - ROW_CHUNK)
    def _():
        w = weights()

        def chunk(c, carry):
            r = pl.ds(pl.multiple_of(c * ROW_CHUNK, ROW_CHUNK), ROW_CHUNK)
            acc_scr[r, :] += _swiglu(xn_scr[r, :], *w)
            return carry

        lax.fori_loop(0, (rows + ROW_CHUNK - 1) // ROW_CHUNK, chunk, 0)


def _ffn_dense_kernel(x_ref, nw_ref, wg_ref, wu_ref, wd_ref, o_ref, xn_scr, acc_scr):
    _swiglu_steps(lambda: x_ref[...], nw_ref, wg_ref, wu_ref, wd_ref, xn_scr, acc_scr)

    @pl.when(pl.program_id(1) == pl.num_programs(1) - 1)
    def _():
        o_ref[...] = x_ref[...] + acc_scr[...]


def _ffn_dense(h, nw, wg, wu, wd, layer):
    T, D = h.shape
    FF = wg.shape[2]
    tm = min(TM_FFN, T)
    tf = min(TF_FFN, FF)
    return pl.pallas_call(
        _ffn_dense_kernel,
        out_shape=jax.ShapeDtypeStruct((T, D), F32),
        grid=(T // tm, FF // tf),
        in_specs=[
            pl.BlockSpec((tm, D), lambda i, j: (i, 0)),
            _const_spec((1, D)),
            pl.BlockSpec((None, D, tf), lambda i, j: (layer, 0, j)),
            pl.BlockSpec((None, D, tf), lambda i, j: (layer, 0, j)),
            pl.BlockSpec((None, tf, D), lambda i, j: (layer, j, 0)),
        ],
        out_specs=pl.BlockSpec((tm, D), lambda i, j: (i, 0)),
        scratch_shapes=[pltpu.VMEM((tm, D), BF16), pltpu.VMEM((tm, D), F32)],
        compiler_params=_params("parallel", "arbitrary"),
        name="ffn_dense",
    )(h, nw.reshape(1, D), wg, wu, wd)


def _rows_to_tile(ref, tm):
    return jnp.concatenate([ref[pl.ds(s, tm, stride=SUBLANES), :] for s in range(SUBLANES)], axis=1)


def _tile_to_rows(ref, x, tm):
    for s in range(SUBLANES):
        ref[pl.ds(s, tm, stride=SUBLANES), :] = x[:, s * LANES:(s + 1) * LANES]


def _ffn_moe_kernel(te_ref, nv_ref, tr_ref, x_ref, nw_ref, wg_ref, wu_ref, wd_ref, o_ref, xn_scr, acc_scr):
    del te_ref
    tm = xn_scr.shape[0]
    valid = pl.program_id(0) < nv_ref[0]
    last = pl.program_id(1) == pl.num_programs(1) - 1

    @pl.when(valid)
    def _():
        _swiglu_steps(lambda: _rows_to_tile(x_ref, tm), nw_ref, wg_ref, wu_ref, wd_ref, xn_scr, acc_scr,
                      rows=tr_ref[pl.program_id(0)])

        @pl.when(last)
        def _():
            _tile_to_rows(o_ref, acc_scr[...], tm)

    @pl.when(jnp.logical_and(jnp.logical_not(valid), last))
    def _():
        o_ref[...] = jnp.zeros(o_ref.shape, F32)


def _ffn_moe(xs, nw, wg, wu, wd, layer, tile_expert, n_valid, tile_rows, tm):
    D = wg.shape[2]
    FF = wg.shape[3]
    P = xs.shape[0] // SUBLANES
    tf = min(TF_FFN, FF)
    nj = FF // tf

    def tile(i, nv):
        return jnp.maximum(jnp.minimum(i, nv[0] - 1), 0)

    def col(i, j, nv):
        return jnp.where(i < nv[0], j, nj - 1)

    def up_map(i, j, te, nv, tr):
        return (layer, te[tile(i, nv)], 0, col(i, j, nv))

    def down_map(i, j, te, nv, tr):
        return (layer, te[tile(i, nv)], col(i, j, nv), 0)

    grid_spec = pltpu.PrefetchScalarGridSpec(
        num_scalar_prefetch=3,
        grid=(P // tm, nj),
        in_specs=[
            pl.BlockSpec((tm * SUBLANES, LANES), lambda i, j, te, nv, tr: (tile(i, nv), 0)),
            pl.BlockSpec((1, D), lambda i, j, te, nv, tr: (0, 0)),
            pl.BlockSpec((None, None, D, tf), up_map),
            pl.BlockSpec((None, None, D, tf), up_map),
            pl.BlockSpec((None, None, tf, D), down_map),
        ],
        out_specs=pl.BlockSpec((tm * SUBLANES, LANES), lambda i, j, te, nv, tr: (i, 0)),
        scratch_shapes=[pltpu.VMEM((tm, D), BF16), pltpu.VMEM((tm, D), F32)],
    )
    return pl.pallas_call(
        _ffn_moe_kernel,
        out_shape=jax.ShapeDtypeStruct((P * SUBLANES, LANES), F32),
        grid_spec=grid_spec,
        compiler_params=_params("arbitrary", "arbitrary"),
        name="ffn_moe",
    )(tile_expert, n_valid, tile_rows, xs, nw.reshape(1, D), wg, wu, wd)


def _lru_kernel(h_ref, nw_ref, win_ref, cw_ref, cb_ref, wax_ref, ba_ref, bx_ref, lam_ref, wout_ref,
                o_ref, halo_scr, carry_scr, a_scr, b_scr, hs_scr):
    tt, W = h_ref.shape[1], a_scr.shape[2]
    G = tt // SUBLANES

    @pl.when(pl.program_id(1) == 0)
    def _():
        halo_scr[...] = jnp.zeros(halo_scr.shape, F32)
        carry_scr[...] = jnp.zeros(carry_scr.shape, F32)

    x = h_ref[0]
    xn = _rmsnorm(x, nw_ref[...]).astype(BF16)
    z = _dot(xn, win_ref[...])
    gate = jax.nn.gelu(z[:, :W])
    xb = z[:, W:]
    xe = jnp.concatenate([halo_scr[...], xb], axis=0)
    halo_scr[...] = xb[tt - SUBLANES:, :]
    cw = cw_ref[...]
    xc = cb_ref[...]
    for kk in range(CONV_WIDTH):
        off = SUBLANES - (CONV_WIDTH - 1) + kk
        xc = xc + cw[kk:kk + 1, :] * xe[off:off + tt, :]
    xcb = xc.astype(BF16)
    hd = W // LRU_HEADS
    r_parts, i_parts = [], []
    for n in range(LRU_HEADS):
        ri = _dot(xcb[:, n * hd:(n + 1) * hd], wax_ref[n])
        r_parts.append(ri[:, :hd])
        i_parts.append(ri[:, hd:])
    r = jax.nn.sigmoid(jnp.concatenate(r_parts, axis=1) + ba_ref[...])
    ig = jax.nn.sigmoid(jnp.concatenate(i_parts, axis=1) + bx_ref[...])
    nl = -lam_ref[...]
    softplus = jnp.maximum(nl, 0.0) + jnp.log1p(jnp.exp(-jnp.abs(nl)))
    a = jnp.exp(-LRU_C * r * softplus)
    b = jnp.sqrt(1.0 - a * a) * (ig * xc)

    A = a.reshape(G, SUBLANES, W)
    Bv = b.reshape(G, SUBLANES, W)
    sub = lax.broadcasted_iota(I32, A.shape, 1)
    for d in (1, 2, 4):
        keep = sub >= d
        a_sh = jnp.where(keep, pltpu.roll(A, d, 1), 1.0)
        b_sh = jnp.where(keep, pltpu.roll(Bv, d, 1), 0.0)
        Bv = A * b_sh + Bv
        A = A * a_sh
    a_scr[...] = A
    b_scr[...] = Bv

    def body(g, hprev):
        hg = b_scr[g] + a_scr[g] * hprev
        hs_scr[g] = hg
        return hg[SUBLANES - 1:, :]

    carry_scr[...] = lax.fori_loop(0, G, body, carry_scr[...], unroll=8)
    y = (hs_scr[...].reshape(tt, W) * gate).astype(BF16)
    o_ref[0] = x + _dot(y, wout_ref[...])


def _lru_mixer(h, nw, w_in, conv_w, conv_b, w_a, b_a, w_x, b_x, lam, w_out):
    B, S, D = h.shape
    W = w_out.shape[0]
    tt = min(TT_LRU, S)
    hd = W // LRU_HEADS
    wax = jnp.concatenate([w_a, w_x], axis=2).astype(BF16)
    G = tt // SUBLANES
    return pl.pallas_call(
        _lru_kernel,
        out_shape=jax.ShapeDtypeStruct((B, S, D), F32),
        grid=(B, S // tt),
        in_specs=[
            pl.BlockSpec((1, tt, D), lambda b, t: (b, t, 0)),
            _const_spec((1, D)),
            _const_spec((D, 2 * W)),
            _const_spec((CONV_WIDTH, W)),
            _const_spec((1, W)),
            _const_spec((LRU_HEADS, hd, 2 * hd)),
            _const_spec((1, W)),
            _const_spec((1, W)),
            _const_spec((1, W)),
            _const_spec((W, D)),
        ],
        out_specs=pl.BlockSpec((1, tt, D), lambda b, t: (b, t, 0)),
        scratch_shapes=[
            pltpu.VMEM((SUBLANES, W), F32), pltpu.VMEM((1, W), F32),
            pltpu.VMEM((G, SUBLANES, W), F32), pltpu.VMEM((G, SUBLANES, W), F32),
            pltpu.VMEM((G, SUBLANES, W), F32),
        ],
        compiler_params=_params("parallel", "arbitrary"),
        name="rglru_mixer",
    )(h, nw.reshape(1, D), w_in.astype(BF16), conv_w, conv_b.reshape(1, W), wax, b_a.reshape(1, W),
      b_x.reshape(1, W), lam.reshape(1, W), w_out.astype(BF16))


def _pool_kernel(h_ref, halo_ref, nw_ref, wg_ref, sc_ref, o_ref, *, blocks_per_seq):
    tm, D = h_ref.shape
    gsz = D // len(POOL_WINDOWS)
    x = h_ref[...]
    first = pl.program_id(0) % blocks_per_seq == 0
    xn = _rmsnorm(x, nw_ref[...])
    hn_halo = _rmsnorm(halo_ref[...], nw_ref[...])
    hn_halo = jnp.where(first, 0.0, hn_halo)
    e = jnp.concatenate([hn_halo, xn], axis=0)
    t = (pl.program_id(0) % blocks_per_seq) * tm + lax.broadcasted_iota(I32, (tm, 1), 0)
    outs = []
    for g, w in enumerate(POOL_WINDOWS):
        cur = e[:, g * gsz:(g + 1) * gsz]
        base = 0
        d = 1
        while d < w:
            cur = cur[d:, :] + cur[:-d, :]
            base += d
            d *= 2
        win = cur[POOL_HALO - base:POOL_HALO - base + tm, :]
        count = jnp.minimum(t + 1, w).astype(F32)
        pooled = (win / count - xn[:, g * gsz:(g + 1) * gsz]).astype(BF16)
        outs.append(_dot(pooled, wg_ref[g]))
    o_ref[...] = x + jnp.concatenate(outs, axis=1) * sc_ref[...]


def _pool_mixer(h, nw, w_grp, scale, S):
    T, D = h.shape
    tm = min(TM_POOL, S)
    bps = S // tm
    hb = tm // POOL_HALO
    ng, gsz = w_grp.shape[0], w_grp.shape[1]
    return pl.pallas_call(
        functools.partial(_pool_kernel, blocks_per_seq=bps),
        out_shape=jax.ShapeDtypeStruct((T, D), F32),
        grid=(T // tm,),
        in_specs=[
            pl.BlockSpec((tm, D), lambda i: (i, 0)),
            pl.BlockSpec((POOL_HALO, D), lambda i: (jnp.maximum(i * hb - 1, 0), 0)),
            _const_spec((1, D)),
            _const_spec((ng, gsz, gsz)),
            _const_spec((1, D)),
        ],
        out_specs=pl.BlockSpec((tm, D), lambda i: (i, 0)),
        compiler_params=_params("parallel"),
        name="pool_mixer",
    )(h, h, nw.reshape(1, D), w_grp.astype(BF16), scale.reshape(1, D))


def _gmlp_kernel(h_ref, nw_ref, win_ref, lg_ref, lb_ref, ws_ref, bs_ref, wout_ref, o_ref):
    tm, D = h_ref.shape
    W = wout_ref.shape[0]
    gw = W // GMLP_GROUPS
    x = h_ref[...]
    xn = _rmsnorm(x, nw_ref[...]).astype(BF16)
    z = jax.nn.gelu(_dot(xn, win_ref[...]))
    u = z[:, :W]
    v = z[:, W:]
    mu = jnp.mean(v, axis=-1, keepdims=True)
    vc = v - mu
    var = jnp.mean(vc * vc, axis=-1, keepdims=True)
    vb = (vc * lax.rsqrt(var + LN_EPS) * lg_ref[...] + lb_ref[...]).astype(BF16)
    row = lax.broadcasted_iota(I32, (GMLP_CHUNK, GMLP_CHUNK), 0)
    col = lax.broadcasted_iota(I32, (GMLP_CHUNK, GMLP_CHUNK), 1)
    bs = bs_ref[...]
    ws = [jnp.where(col <= row, ws_ref[g], 0.0).astype(BF16) for g in range(GMLP_GROUPS)]
    rows = []
    for c in range(tm // GMLP_CHUNK):
        cols = []
        for g in range(GMLP_GROUPS):
            blk = vb[c * GMLP_CHUNK:(c + 1) * GMLP_CHUNK, g * gw:(g + 1) * gw]
            cols.append(_dot(ws[g], blk) + bs[:, g:g + 1])
        rows.append(jnp.concatenate(cols, axis=1))
    v2 = jnp.concatenate(rows, axis=0)
    o_ref[...] = x + _dot((u * v2).astype(BF16), wout_ref[...])


def _gmlp_mixer(h, nw, w_in, ln_g, ln_b, w_s, b_s, w_out):
    T, D = h.shape
    W = w_out.shape[0]
    tm = min(TM_GMLP, T)
    return pl.pallas_call(
        _gmlp_kernel,
        out_shape=jax.ShapeDtypeStruct((T, D), F32),
        grid=(T // tm,),
        in_specs=[
            pl.BlockSpec((tm, D), lambda i: (i, 0)),
            _const_spec((1, D)),
            _const_spec((D, 2 * W)),
            _const_spec((1, W)),
            _const_spec((1, W)),
            _const_spec((GMLP_GROUPS, GMLP_CHUNK, GMLP_CHUNK)),
            _const_spec((GMLP_CHUNK, GMLP_GROUPS)),
            _const_spec((W, D)),
        ],
        out_specs=pl.BlockSpec((tm, D), lambda i: (i, 0)),
        compiler_params=_params("parallel"),
        name="gmlp_mixer",
    )(h, nw.reshape(1, D), w_in.astype(BF16), ln_g.reshape(1, W), ln_b.reshape(1, W), w_s, b_s.T,
      w_out.astype(BF16))


def _router_kernel(h_ref, nw_ref, whi_ref, wlo_ref, meta_ref, gate_ref, cnt_ref, cnt_scr):
    tm = h_ref.shape[0]

    @pl.when(pl.program_id(0) == 0)
    def _():
        cnt_scr[...] = jnp.zeros(cnt_scr.shape, F32)

    xn = _rmsnorm(h_ref[...], nw_ref[...])
    x_hi = xn.astype(BF16)
    x_lo = (xn - x_hi.astype(F32)).astype(BF16)
    nt = (((1,), (1,)), ((), ()))
    logits = (lax.dot_general(whi_ref[...], x_hi, nt, preferred_element_type=F32)
              + lax.dot_general(whi_ref[...], x_lo, nt, preferred_element_type=F32)
              + lax.dot_general(wlo_ref[...], x_hi, nt, preferred_element_type=F32))
    eid = lax.broadcasted_iota(I32, logits.shape, 0).astype(F32)
    m1 = jnp.max(logits, axis=0, keepdims=True)
    e1 = jnp.min(jnp.where(logits == m1, eid, float(N_EXPERTS)), axis=0, keepdims=True)
    rest = jnp.where(eid == e1, -jnp.inf, logits)
    m2 = jnp.max(rest, axis=0, keepdims=True)
    e2 = jnp.min(jnp.where(rest == m2, eid, float(N_EXPERTS)), axis=0, keepdims=True)
    ex = jnp.exp(m2 - m1)
    g1 = 1.0 / (1.0 + ex)
    g2 = ex / (1.0 + ex)
    hit1 = eid == e1
    hit2 = eid == e2
    onehot = jnp.where(hit1 | hit2, 1.0, 0.0)
    row = lax.broadcasted_iota(I32, (tm, tm), 0)
    col = lax.broadcasted_iota(I32, (tm, tm), 1)
    earlier = jnp.where(row < col, 1.0, 0.0).astype(BF16)
    before = _dot(onehot.astype(BF16), earlier) + cnt_scr[...]
    r1 = jnp.sum(jnp.where(hit1, before, 0.0), axis=0, keepdims=True)
    r2 = jnp.sum(jnp.where(hit2, before, 0.0), axis=0, keepdims=True)
    cnt_scr[...] += jnp.sum(onehot, axis=1, keepdims=True)
    zero = jnp.zeros((SUBLANES - 4, tm), F32)
    meta_ref[...] = jnp.concatenate([e1, e2, r1, r2, zero], axis=0).astype(I32)
    gate_ref[...] = jnp.concatenate([g1, g2, zero, 0.0 * g1, 0.0 * g2], axis=0)
    cnt_ref[...] = jnp.broadcast_to(cnt_scr[...], cnt_ref.shape)


def _router(h, nw, w_router):
    T, D = h.shape
    assert N_EXPERTS == SUBLANES
    tm = min(TM_ROUTE, T)
    tok_spec = pl.BlockSpec((SUBLANES, tm), lambda i: (0, i))
    w_hi = w_router.T.astype(BF16)
    w_lo = (w_router.T - w_hi.astype(F32)).astype(BF16)
    return pl.pallas_call(
        _router_kernel,
        out_shape=(jax.ShapeDtypeStruct((SUBLANES, T), I32), jax.ShapeDtypeStruct((SUBLANES, T), F32),
                   jax.ShapeDtypeStruct((N_EXPERTS, LANES), F32)),
        grid=(T // tm,),
        in_specs=[pl.BlockSpec((tm, D), lambda i: (i, 0)), _const_spec((1, D)), _const_spec((N_EXPERTS, D)),
                  _const_spec((N_EXPERTS, D))],
        out_specs=(tok_spec, tok_spec, _const_spec((N_EXPERTS, LANES))),
        scratch_shapes=[pltpu.VMEM((N_EXPERTS, 1), F32)],
        compiler_params=_params("arbitrary"),
        name="moe_router",
    )(h, nw.reshape(1, D), w_hi, w_lo)


ZERO_ROWS = 512


def _dispatch_kernel(zs_ref, zl_ref, d1_ref, d2_ref, h_ref, xs_ref, rows_scr, zero_scr, sem, zsem):
    tm = h_ref.shape[0]
    _tile_to_rows(rows_scr, h_ref[...], tm)

    def copies(r):
        src = rows_scr.at[pl.ds(pl.multiple_of(r * SUBLANES, SUBLANES), SUBLANES), :]
        return [pltpu.make_async_copy(
            src, xs_ref.at[pl.ds(pl.multiple_of(d[0, 0, r] * SUBLANES, SUBLANES), SUBLANES), :], sem)
            for d in (d1_ref, d2_ref)]

    def start(r, c):
        for prio, cp in enumerate(copies(r)):
            cp.start(priority=prio)
        return c

    def wait(r, c):
        for cp in copies(r):
            cp.wait()
        return c

    def zero_copies(fn):
        def zero_dma(off, size, pred):
            cp = pltpu.make_async_copy(
                zero_scr.at[pl.ds(0, size * SUBLANES), :],
                xs_ref.at[pl.ds(pl.multiple_of(off * SUBLANES, SUBLANES), size * SUBLANES), :], zsem)
            pl.when(pred)(functools.partial(fn, cp))

        for k in range(N_EXPERTS + 1):
            tail = k == N_EXPERTS
            max_rows = 2 * ZERO_ROWS * (N_EXPERTS if tail else 1)
            n = zl_ref[k]
            whole = n // ZERO_ROWS
            for c in range(max_rows // ZERO_ROWS):
                zero_dma(zs_ref[k] + c * ZERO_ROWS, ZERO_ROWS, c < whole)
            base = zs_ref[k] + whole * ZERO_ROWS
            rest = n % ZERO_ROWS
            size = ZERO_ROWS // 2
            while size >= 1:
                zero_dma(base + (rest // (2 * size)) * (2 * size), size, (rest // size) % 2 == 1)
                size //= 2

    first = pl.program_id(0) == 0

    @pl.when(first)
    def _():
        zero_scr[...] = jnp.zeros(zero_scr.shape, F32)
        zero_copies(lambda cp: cp.start())

    lax.fori_loop(0, tm, start, 0, unroll=4)
    lax.fori_loop(0, tm, wait, 0, unroll=4)

    @pl.when(first)
    def _():
        zero_copies(lambda cp: cp.wait())


def _dispatch(h, d1, d2, zero_start, zero_len, P):
    T, D = h.shape
    tm = min(TM_DISPATCH, T)
    nb = T // tm
    smem = functools.partial(pl.BlockSpec, memory_space=pltpu.SMEM)
    grid_spec = pltpu.PrefetchScalarGridSpec(
        num_scalar_prefetch=2,
        grid=(nb,),
        in_specs=[
            smem((1, 1, tm), lambda i, zs, zl: (i, 0, 0)),
            smem((1, 1, tm), lambda i, zs, zl: (i, 0, 0)),
            pl.BlockSpec((tm, D), lambda i, zs, zl: (i, 0)),
        ],
        out_specs=pl.BlockSpec(memory_space=pl.ANY),
        scratch_shapes=[pltpu.VMEM((tm * SUBLANES, LANES), F32), pltpu.VMEM((ZERO_ROWS * SUBLANES, LANES), F32),
                        pltpu.SemaphoreType.DMA, pltpu.SemaphoreType.DMA],
    )
    return pl.pallas_call(
        _dispatch_kernel,
        out_shape=jax.ShapeDtypeStruct((P * SUBLANES, LANES), F32),
        grid_spec=grid_spec,
        compiler_params=_params("arbitrary"),
        name="moe_dispatch",
    )(zero_start, zero_len, d1.reshape(nb, 1, tm), d2.reshape(nb, 1, tm), h)


def _combine_kernel(d1_ref, d2_ref, n1_ref, n2_ref, h_ref, gate_ref, ys_ref, *rest, final_norm):
    if final_norm:
        fw_ref, o_ref, buf1, buf2, sem = rest
    else:
        o_ref, buf1, buf2, sem = rest
    tm = h_ref.shape[0]
    step = pl.program_id(0)
    slot = step % 2

    def copies(srcs, buf, r):
        dst = pl.ds(pl.multiple_of(r * SUBLANES, SUBLANES), SUBLANES)
        return [pltpu.make_async_copy(
            ys_ref.at[pl.ds(pl.multiple_of(d[0, 0, r] * SUBLANES, SUBLANES), SUBLANES), :],
            b.at[buf, dst, :], sem.at[buf]) for d, b in zip(srcs, (buf1, buf2))]

    def issue(srcs, buf):
        def start(r, c):
            for prio, cp in enumerate(copies(srcs, buf, r)):
                cp.start(priority=prio)
            return c

        lax.fori_loop(0, tm, start, 0, unroll=4)

    @pl.when(step == 0)
    def _():
        issue((d1_ref, d2_ref), slot)

    @pl.when(step + 1 < pl.num_programs(0))
    def _():
        issue((n1_ref, n2_ref), 1 - slot)

    def wait(r, c):
        for cp in copies((d1_ref, d2_ref), slot, r):
            cp.wait()
        return c

    lax.fori_loop(0, tm, wait, 0, unroll=4)
    gates = gate_ref[...]
    out = (h_ref[...] + gates[:, 0:1] * _rows_to_tile(buf1.at[slot], tm)
           + gates[:, 1:2] * _rows_to_tile(buf2.at[slot], tm))
    if final_norm:
        out = _rmsnorm(out, fw_ref[...])
    o_ref[...] = out


def _combine(h, gates, ys, d1, d2, final_w=None):
    T, D = h.shape
    tm = min(TM_COMBINE, T)
    nb = T // tm
    smem = functools.partial(pl.BlockSpec, memory_space=pltpu.SMEM)
    in_specs = [
        smem((1, 1, tm), lambda i: (i, 0, 0)),
        smem((1, 1, tm), lambda i: (i, 0, 0)),
        smem((1, 1, tm), lambda i: (jnp.minimum(i + 1, nb - 1), 0, 0)),
        smem((1, 1, tm), lambda i: (jnp.minimum(i + 1, nb - 1), 0, 0)),
        pl.BlockSpec((tm, D), lambda i: (i, 0)),
        pl.BlockSpec((tm, TOP_K), lambda i: (i, 0)),
        pl.BlockSpec(memory_space=pl.ANY),
    ]
    d1 = d1.reshape(nb, 1, tm)
    d2 = d2.reshape(nb, 1, tm)
    args = [d1, d2, d1, d2, h, gates, ys]
    if final_w is not None:
        in_specs.append(_const_spec((1, D)))
        args.append(final_w.reshape(1, D))
    buf = pltpu.VMEM((2, tm * SUBLANES, LANES), F32)
    return pl.pallas_call(
        functools.partial(_combine_kernel, final_norm=final_w is not None),
        out_shape=jax.ShapeDtypeStruct((T, D), F32),
        grid=(nb,),
        in_specs=in_specs,
        out_specs=pl.BlockSpec((tm, D), lambda i: (i, 0)),
        scratch_shapes=[buf, buf, pltpu.SemaphoreType.DMA((2,))],
        compiler_params=_params("arbitrary"),
        name="moe_combine",
    )(*args)


def _moe_layer(h, nw, w_router, wg, wu, wd, layer, final_w=None):
    T, D = h.shape
    meta, gates, counts = _router(h, nw, w_router)
    tm = min(TM_MOE, T)
    sizes = counts[:, 0].astype(I32)
    padded = (sizes + tm - 1) // tm * tm
    pad_end = jnp.cumsum(padded)
    pad_start = pad_end - padded
    d1 = pad_start[meta[0]] + meta[2]
    d2 = pad_start[meta[1]] + meta[3]
    n_tiles = (T * TOP_K + N_EXPERTS * (tm - 1)) // tm
    tile_start = jnp.arange(n_tiles, dtype=I32) * tm
    tile_expert = jnp.minimum(jnp.sum(pad_end[None, :] <= tile_start[:, None], axis=1), N_EXPERTS - 1)
    n_valid = (pad_end[-1] // tm).reshape(1)
    assert tm <= 2 * ZERO_ROWS
    total = n_tiles * tm
    zero_start = jnp.concatenate([pad_start + sizes, pad_end[-1:]])
    zero_len = jnp.concatenate([padded - sizes, total - pad_end[-1:]])
    xs = _dispatch(h, d1, d2, zero_start, zero_len, total)
    tile_rows = jnp.clip((pad_start + sizes)[tile_expert] - tile_start, 0, tm)
    ys = _ffn_moe(xs, nw, wg, wu, wd, layer, tile_expert.astype(I32), n_valid.astype(I32),
                  tile_rows.astype(I32), tm)
    return _combine(h, gates[:TOP_K].T, ys, d1, d2, final_w)


def kernel(x, positions, norm_mix, norm_ffn, norm_final, mla_w_in, mla_q_norm, mla_kv_norm, mla_w_uq,
           mla_w_ukv, mla_w_o, lru_w_in, lru_conv_w, lru_conv_b, lru_w_a, lru_b_a, lru_w_x, lru_b_x,
           lru_lam, lru_w_out, pool_w_grp, pool_scale, gmlp_w_in, gmlp_ln_g, gmlp_ln_b, gmlp_w_s,
           gmlp_b_s, gmlp_w_out, ffn_w_gate, ffn_w_up, ffn_w_down, moe_w_router, moe_w_gate, moe_w_up,
           moe_w_down):
    B, S, D = x.shape
    T = B * S
    depth = norm_mix.shape[0]
    h = x.reshape(T, D)
    for i in range(depth):
        m, j = i % 4, i // 4
        if m == 0:
            q, k, v = _mla_proj(h, positions.reshape(1, T), norm_mix[i], mla_w_in[j], mla_q_norm[j],
                                mla_kv_norm[j], mla_w_uq[j], mla_w_ukv[j], B, S)
            o = _attention(q, k, v)
            h = _matmul_residual(o.reshape(T, -1), mla_w_o[j], h)
        elif m == 1:
            h = _lru_mixer(h.reshape(B, S, D), norm_mix[i], lru_w_in[j], lru_conv_w[j], lru_conv_b[j],
                           lru_w_a[j], lru_b_a[j], lru_w_x[j], lru_b_x[j], lru_lam[j],
                           lru_w_out[j]).reshape(T, D)
        elif m == 2:
            h = _pool_mixer(h, norm_mix[i], pool_w_grp[j], pool_scale[j], S)
        else:
            h = _gmlp_mixer(h, norm_mix[i], gmlp_w_in[j], gmlp_ln_g[j], gmlp_ln_b[j], gmlp_w_s[j],
                            gmlp_b_s[j], gmlp_w_out[j])
        kk = i // 2
        if i % 2 == 0:
            h = _ffn_dense(h, norm_ffn[i], ffn_w_gate, ffn_w_up, ffn_w_down, kk)
        else:
            last = i == depth - 1
            h = _moe_layer(h, norm_ffn[i], moe_w_router[kk], moe_w_gate, moe_w_up, moe_w_down, kk,
                           norm_final if last else None)
    if depth % 2 == 1:
        raise NotImplementedError("final RMSNorm is fused into the last MoE combine")
    return h.reshape(B, S, D)
```

```python
import functools

import jax
import jax.numpy as jnp
from jax import lax
from jax.experimental import pallas as pl
from jax.experimental.pallas import tpu as pltpu

F32 = jnp.float32
BF16 = jnp.bfloat16
I32 = jnp.int32

RMS_EPS = 1e-6
LN_EPS = 1e-5

MLA_HEADS = 16
MLA_Q_RANK = 384
MLA_KV_RANK = 256
MLA_NOPE = 64
MLA_ROPE = 32
MLA_V = 64
ROPE_BASE = 10000.0
LRU_HEADS = 8
CONV_WIDTH = 4
LRU_C = 8.0
POOL_WINDOWS = (2, 4, 8, 16)
GMLP_GROUPS = 4
GMLP_CHUNK = 128
N_EXPERTS = 8
TOP_K = 2

LANES = 128
SUBLANES = 8
VMEM_LIMIT_BYTES = 56 * 1024 * 1024

TM_PROJ = 512
TQ_ATTN = 512
HEADS_PER_STEP = 8
SUM_ROWS = 16
TM_RES = 512
TM_FFN = 1024
TF_FFN = 512
ROW_CHUNK = 256
TT_LRU = 512
TM_POOL = 512
TM_GMLP = 512
TM_ROUTE = 512
TM_MOE = 1024
TM_DISPATCH = 512
TM_COMBINE = 512
POOL_HALO = 16


def _params(*sem):
    return pltpu.CompilerParams(dimension_semantics=sem, vmem_limit_bytes=VMEM_LIMIT_BYTES)


def _rmsnorm(x, g):
    return x * lax.rsqrt(jnp.mean(x * x, axis=-1, keepdims=True) + RMS_EPS) * g


def _dot(a, b):
    return jnp.dot(a, b, preferred_element_type=F32)


def _const_spec(shape):
    return pl.BlockSpec(shape, lambda *_: (0,) * len(shape))


def _mla_proj_kernel(h_ref, pos_ref, nw_ref, win_ref, qn_ref, kvn_ref, wuqt_ref, wuk_ref, wuvt_ref,
                     invf_ref, q_ref, k_ref, v_ref, *, scale):
    half = MLA_ROPE // 2
    x = h_ref[...]
    tm = x.shape[0]
    xn = _rmsnorm(x, nw_ref[...]).astype(BF16)
    lat = _dot(xn, win_ref[...])
    cq = _rmsnorm(lat[:, :MLA_Q_RANK], qn_ref[...]).astype(BF16)
    ckv = _rmsnorm(lat[:, MLA_Q_RANK:MLA_Q_RANK + MLA_KV_RANK], kvn_ref[...]).astype(BF16)
    kr = lat[:, MLA_Q_RANK + MLA_KV_RANK:]
    ang = invf_ref[...] * pos_ref[...].astype(F32)
    c = jnp.cos(ang)
    s = jnp.sin(ang)
    pad_r = LANES - MLA_NOPE - MLA_ROPE
    one = jnp.ones((MLA_NOPE, tm), F32)
    c_t = jnp.concatenate([one, c, c, jnp.ones((pad_r, tm), F32)], axis=0).T
    s_lo = jnp.concatenate([0.0 * one, -s, jnp.zeros((half + pad_r, tm), F32)], axis=0).T
    s_hi = jnp.concatenate([0.0 * one, 0.0 * s, s, jnp.zeros((pad_r, tm), F32)], axis=0).T
    kr = kr * c_t + pltpu.roll(kr, LANES - half, 1) * s_lo + pltpu.roll(kr, half, 1) * s_hi
    kn = _dot(ckv, wuk_ref[...])
    for hh in range(MLA_HEADS):
        k_ref[0, hh] = (kn[:, hh * LANES:(hh + 1) * LANES] + kr).astype(BF16)
    nt = (((1,), (1,)), ((), ()))
    qt = lax.dot_general(wuqt_ref[...], cq, nt, preferred_element_type=F32)
    for hh in range(MLA_HEADS):
        blk = qt[hh * LANES:(hh + 1) * LANES]
        x1 = blk[MLA_NOPE:MLA_NOPE + half]
        x2 = blk[MLA_NOPE + half:MLA_NOPE + MLA_ROPE]
        roped = jnp.concatenate(
            [blk[:MLA_NOPE], x1 * c - x2 * s, x2 * c + x1 * s, blk[MLA_NOPE + MLA_ROPE:]], axis=0)
        q_ref[0, hh] = (roped * scale).astype(BF16)
    v_ref[0, 0] = lax.dot_general(wuvt_ref[...], ckv, nt, preferred_element_type=F32).astype(BF16)


def _mla_proj(h, pos, nw, w_in, q_norm, kv_norm, w_uq, w_ukv, B, S):
    T, D = h.shape
    H = MLA_HEADS
    tm = min(TM_PROJ, S)
    ns = S // tm
    qk = MLA_NOPE + MLA_ROPE
    pad_r = LANES - qk
    w_in_p = jnp.concatenate([
        w_in[:, :MLA_Q_RANK + MLA_KV_RANK],
        jnp.zeros((D, MLA_NOPE), F32), w_in[:, MLA_Q_RANK + MLA_KV_RANK:], jnp.zeros((D, pad_r), F32)],
        axis=1).astype(BF16)
    wq = w_uq.reshape(MLA_Q_RANK, H, qk)
    wq_t = jnp.pad(wq, ((0, 0), (0, 0), (0, pad_r))).reshape(MLA_Q_RANK, H * LANES).T.astype(BF16)
    wkv = w_ukv.reshape(MLA_KV_RANK, H, MLA_NOPE + MLA_V)
    wk_p = jnp.pad(wkv[:, :, :MLA_NOPE], ((0, 0), (0, 0), (0, LANES - MLA_NOPE)))
    wk_p = wk_p.reshape(MLA_KV_RANK, H * LANES).astype(BF16)
    wv_t = wkv[:, :, MLA_NOPE:].reshape(MLA_KV_RANK, H * MLA_V).T.astype(BF16)
    half = MLA_ROPE // 2
    invf = (ROPE_BASE ** (-jnp.arange(half, dtype=F32) / half)).reshape(half, 1)
    wl = w_in_p.shape[1]
    out_shape = (
        jax.ShapeDtypeStruct((B, H, LANES, S), BF16),
        jax.ShapeDtypeStruct((B, H, S, LANES), BF16),
        jax.ShapeDtypeStruct((B, ns, H * MLA_V, tm), BF16),
    )
    scale = qk ** -0.5 * 1.4426950408889634
    return pl.pallas_call(
        functools.partial(_mla_proj_kernel, scale=scale),
        out_shape=out_shape,
        grid=(T // tm,),
        in_specs=[
            pl.BlockSpec((tm, D), lambda i: (i, 0)),
            pl.BlockSpec((1, tm), lambda i: (0, i)),
            _const_spec((1, D)),
            _const_spec((D, wl)),
            _const_spec((1, MLA_Q_RANK)),
            _const_spec((1, MLA_KV_RANK)),
            _const_spec((H * LANES, MLA_Q_RANK)),
            _const_spec((MLA_KV_RANK, H * LANES)),
            _const_spec((H * MLA_V, MLA_KV_RANK)),
            _const_spec((half, 1)),
        ],
        out_specs=(pl.BlockSpec((1, H, LANES, tm), lambda i: (i // ns, 0, 0, i % ns)),
                   pl.BlockSpec((1, H, tm, LANES), lambda i: (i // ns, 0, i % ns, 0)),
                   pl.BlockSpec((1, 1, H * MLA_V, tm), lambda i: (i // ns, i % ns, 0, 0))),
        compiler_params=_params("parallel"),
        name="mla_proj",
    )(h, pos, nw.reshape(1, D), w_in_p, q_norm.reshape(1, -1), kv_norm.reshape(1, -1), wq_t, wk_p, wv_t,
      invf)


def _attn_kernel(q_ref, k_ref, v_ref, o_ref, m_scr, acc_scr, s_scr, *, tq):
    qi = pl.program_id(2)
    nh = q_ref.shape[1]
    m_scr[...] = jnp.full(m_scr.shape, -jnp.inf, F32)
    acc_scr[...] = jnp.zeros(acc_scr.shape, F32)

    def scores(j, hh):
        return _dot(k_ref[0, hh, pl.ds(pl.multiple_of(j * tq, tq), tq), :], q_ref[0, hh])

    ones = jnp.ones((SUM_ROWS, tq), BF16)

    def softmax_pv(s, j, hh, masked):
        if masked:
            kv = lax.broadcasted_iota(I32, s.shape, 0)
            qq = lax.broadcasted_iota(I32, s.shape, 1)
            s = jnp.where(kv <= qq, s, -jnp.inf)
        m_prev = m_scr[hh]
        m_new = jnp.maximum(m_prev, jnp.max(s, axis=0, keepdims=True))
        alpha = jnp.exp2(m_prev - m_new)
        p = jnp.exp2(s - m_new).astype(BF16)
        vt = jnp.concatenate([v_ref[0, j, hh * MLA_V:(hh + 1) * MLA_V, :], ones], axis=0)
        acc_scr[hh] = alpha * acc_scr[hh] + _dot(vt, p)
        m_scr[hh] = m_new

    def stage(j, slot, masked, prefetch):
        for hh in range(nh):
            s = s_scr[slot, hh]
            if prefetch:
                s_scr[1 - slot, hh] = scores(j + 1, hh)
            softmax_pv(s, j, hh, masked)

    for hh in range(nh):
        s_scr[0, hh] = scores(0, hh)

    def pair(i, carry):
        stage(2 * i, 0, False, True)
        stage(2 * i + 1, 1, False, True)
        return carry

    lax.fori_loop(0, qi // 2, pair, 0)

    @pl.when(qi % 2 == 1)
    def _():
        stage(qi - 1, 0, False, True)
        stage(qi, 1, True, False)

    @pl.when(qi % 2 == 0)
    def _():
        stage(qi, 0, True, False)
    out = jnp.concatenate([acc_scr[hh, :MLA_V] / acc_scr[hh, MLA_V:MLA_V + 1] for hh in range(nh)], axis=0)
    o_ref[0] = out.T.astype(BF16)


def _attention(qt, k, vt):
    B, H, S, _ = k.shape
    tq = vt.shape[3]
    nh = HEADS_PER_STEP
    return pl.pallas_call(
        functools.partial(_attn_kernel, tq=tq),
        out_shape=jax.ShapeDtypeStruct((B, S, H * MLA_V), BF16),
        grid=(B, H // nh, S // tq),
        in_specs=[
            pl.BlockSpec((1, nh, LANES, tq), lambda b, hp, qi: (b, hp, 0, qi)),
            pl.BlockSpec((1, nh, S, LANES), lambda b, hp, qi: (b, hp, 0, 0)),
            pl.BlockSpec((1, S // tq, nh * MLA_V, tq), lambda b, hp, qi: (b, 0, hp, 0)),
        ],
        out_specs=pl.BlockSpec((1, tq, nh * MLA_V), lambda b, hp, qi: (b, qi, hp)),
        scratch_shapes=[pltpu.VMEM((nh, 1, tq), F32), pltpu.VMEM((nh, MLA_V + SUM_ROWS, tq), F32),
                        pltpu.VMEM((2, nh, tq, tq), F32)],
        compiler_params=_params("parallel", "parallel", "arbitrary"),
        name="mla_attention",
    )(qt, k, vt)


def _matmul_res_kernel(x_ref, w_ref, h_ref, o_ref):
    o_ref[...] = h_ref[...] + _dot(x_ref[...], w_ref[...])


def _matmul_residual(x, w, h):
    T, K = x.shape
    N = w.shape[1]
    tm = min(TM_RES, T)
    return pl.pallas_call(
        _matmul_res_kernel,
        out_shape=jax.ShapeDtypeStruct((T, N), F32),
        grid=(T // tm,),
        in_specs=[pl.BlockSpec((tm, K), lambda i: (i, 0)), _const_spec((K, N)),
                  pl.BlockSpec((tm, N), lambda i: (i, 0))],
        out_specs=pl.BlockSpec((tm, N), lambda i: (i, 0)),
        compiler_params=_params("parallel"),
        name="matmul_residual",
    )(x, w.astype(BF16), h)


def _swiglu(xn, wg, wu, wd, parts=1):
    width = wg.shape[1] // parts
    out = None
    hm = []
    for c in range(parts):
        cols = slice(c * width, (c + 1) * width)
        g = _dot(xn, wg[:, cols])
        u = _dot(xn, wu[:, cols])
        hm.append((g * jax.nn.sigmoid(g) * u).astype(BF16))
    for c in range(parts):
        y = _dot(hm[c], wd[c * width:(c + 1) * width])
        out = y if out is None else out + y
    return out


def _swiglu_steps(load_x, nw_ref, wg_ref, wu_ref, wd_ref, xn_scr, acc_scr, rows=None):
    j = pl.program_id(1)
    tm = xn_scr.shape[0]

    @pl.when(j == 0)
    def _():
        xn_scr[...] = _rmsnorm(load_x(), nw_ref[...]).astype(BF16)
        acc_scr[...] = jnp.zeros(acc_scr.shape, F32)

    def weights():
        return wg_ref[...].astype(BF16), wu_ref[...].astype(BF16), wd_ref[...].astype(BF16)

    def whole():
        acc_scr[...] += _swiglu(xn_scr[...], *weights(), parts=2)

    if rows is None:
        whole()
        return
    half = tm // 2
    if half != 2 * ROW_CHUNK:
        whole()
        return
    chunks = (rows + ROW_CHUNK - 1) // ROW_CHUNK

    def piece(start, size, parts):
        acc_scr[start:start + size, :] += _swiglu(xn_scr[start:start + size, :], *weights(), parts=parts)

    pl.when(chunks == 4)(whole)

    @pl.when(chunks == 3)
    def _():
        piece(0, half, 2)
        piece(half, ROW_CHUNK, 1)

    pl.when(chunks == 2)(functools.partial(piece, 0, half, 2))
    pl.when(chunks == 1)(functools.partial(piece, 0, ROW_CHUNK, 1))


def _ffn_dense_kernel(x_ref, nw_ref, wg_ref, wu_ref, wd_ref, o_ref, xn_scr, acc_scr):
    _swiglu_steps(lambda: x_ref[...], nw_ref, wg_ref, wu_ref, wd_ref, xn_scr, acc_scr)

    @pl.when(pl.program_id(1) == pl.num_programs(1) - 1)
    def _():
        o_ref[...] = x_ref[...] + acc_scr[...]


def _ffn_dense(h, nw, wg, wu, wd, layer):
    T, D = h.shape
    FF = wg.shape[2]
    tm = min(TM_FFN, T)
    tf = min(TF_FFN, FF)
    return pl.pallas_call(
        _ffn_dense_kernel,
        out_shape=jax.ShapeDtypeStruct((T, D), F32),
        grid=(T // tm, FF // tf),
        in_specs=[
            pl.BlockSpec((tm, D), lambda i, j: (i, 0)),
            _const_spec((1, D)),
            pl.BlockSpec((None, D, tf), lambda i, j: (layer, 0, j)),
            pl.BlockSpec((None, D, tf), lambda i, j: (layer, 0, j)),
            pl.BlockSpec((None, tf, D), lambda i, j: (layer, j, 0)),
        ],
        out_specs=pl.BlockSpec((tm, D), lambda i, j: (i, 0)),
        scratch_shapes=[pltpu.VMEM((tm, D), BF16), pltpu.VMEM((tm, D), F32)],
        compiler_params=_params("parallel", "arbitrary"),
        name="ffn_dense",
    )(h, nw.reshape(1, D), wg, wu, wd)


def _rows_to_tile(ref, tm):
    return jnp.concatenate([ref[pl.ds(s, tm, stride=SUBLANES), :] for s in range(SUBLANES)], axis=1)


def _tile_to_rows(ref, x, tm):
    for s in range(SUBLANES):
        ref[pl.ds(s, tm, stride=SUBLANES), :] = x[:, s * LANES:(s + 1) * LANES]


def _ffn_moe_kernel(te_ref, nv_ref, tr_ref, x_ref, nw_ref, wg_ref, wu_ref, wd_ref, o_ref, xn_scr, acc_scr):
    del te_ref
    tm = xn_scr.shape[0]
    valid = pl.program_id(0) < nv_ref[0]
    last = pl.program_id(1) == pl.num_programs(1) - 1

    @pl.when(valid)
    def _():
        _swiglu_steps(lambda: _rows_to_tile(x_ref, tm), nw_ref, wg_ref, wu_ref, wd_ref, xn_scr, acc_scr,
                      rows=tr_ref[pl.program_id(0)])

        @pl.when(last)
        def _():
            _tile_to_rows(o_ref, acc_scr[...], tm)

    @pl.when(jnp.logical_and(jnp.logical_not(valid), last))
    def _():
        o_ref[...] = jnp.zeros(o_ref.shape, F32)


def _ffn_moe(xs, nw, wg, wu, wd, layer, tile_expert, n_valid, tile_rows, tm):
    D = wg.shape[2]
    FF = wg.shape[3]
    P = xs.shape[0] // SUBLANES
    tf = min(TF_FFN, FF)
    nj = FF // tf

    def tile(i, nv):
        return jnp.maximum(jnp.minimum(i, nv[0] - 1), 0)

    def col(i, j, nv):
        return jnp.where(i < nv[0], j, nj - 1)

    def up_map(i, j, te, nv, tr):
        return (layer, te[tile(i, nv)], 0, col(i, j, nv))

    def down_map(i, j, te, nv, tr):
        return (layer, te[tile(i, nv)], col(i, j, nv), 0)

    grid_spec = pltpu.PrefetchScalarGridSpec(
        num_scalar_prefetch=3,
        grid=(P // tm, nj),
        in_specs=[
            pl.BlockSpec((tm * SUBLANES, LANES), lambda i, j, te, nv, tr: (tile(i, nv), 0)),
            pl.BlockSpec((1, D), lambda i, j, te, nv, tr: (0, 0)),
            pl.BlockSpec((None, None, D, tf), up_map),
            pl.BlockSpec((None, None, D, tf), up_map),
            pl.BlockSpec((None, None, tf, D), down_map),
        ],
        out_specs=pl.BlockSpec((tm * SUBLANES, LANES), lambda i, j, te, nv, tr: (i, 0)),
        scratch_shapes=[pltpu.VMEM((tm, D), BF16), pltpu.VMEM((tm, D), F32)],
    )
    return pl.pallas_call(
        _ffn_moe_kernel,
        out_shape=jax.ShapeDtypeStruct((P * SUBLANES, LANES), F32),
        grid_spec=grid_spec,
        compiler_params=_params("arbitrary", "arbitrary"),
        name="ffn_moe",
    )(tile_expert, n_valid, tile_rows, xs, nw.reshape(1, D), wg, wu, wd)


def _lru_kernel(h_ref, nw_ref, win_ref, cw_ref, cb_ref, wax_ref, ba_ref, bx_ref, lam_ref, wout_ref,
                o_ref, halo_scr, carry_scr, a_scr, b_scr, hs_scr):
    tt, W = h_ref.shape[1], a_scr.shape[2]
    G = tt // SUBLANES

    @pl.when(pl.program_id(1) == 0)
    def _():
        halo_scr[...] = jnp.zeros(halo_scr.shape, F32)
        carry_scr[...] = jnp.zeros(carry_scr.shape, F32)

    x = h_ref[0]
    xn = _rmsnorm(x, nw_ref[...]).astype(BF16)
    z = _dot(xn, win_ref[...])
    gate = jax.nn.gelu(z[:, :W])
    xb = z[:, W:]
    xe = jnp.concatenate([halo_scr[...], xb], axis=0)
    halo_scr[...] = xb[tt - SUBLANES:, :]
    cw = cw_ref[...]
    xc = cb_ref[...]
    for kk in range(CONV_WIDTH):
        off = SUBLANES - (CONV_WIDTH - 1) + kk
        xc = xc + cw[kk:kk + 1, :] * xe[off:off + tt, :]
    xcb = xc.astype(BF16)
    hd = W // LRU_HEADS
    r_parts, i_parts = [], []
    for n in range(LRU_HEADS):
        ri = _dot(xcb[:, n * hd:(n + 1) * hd], wax_ref[n])
        r_parts.append(ri[:, :hd])
        i_parts.append(ri[:, hd:])
    r = jax.nn.sigmoid(jnp.concatenate(r_parts, axis=1) + ba_ref[...])
    ig = jax.nn.sigmoid(jnp.concatenate(i_parts, axis=1) + bx_ref[...])
    nl = -lam_ref[...]
    softplus = jnp.maximum(nl, 0.0) + jnp.log1p(jnp.exp(-jnp.abs(nl)))
    a = jnp.exp(-LRU_C * r * softplus)
    b = jnp.sqrt(1.0 - a * a) * (ig * xc)

    A = a.reshape(G, SUBLANES, W)
    Bv = b.reshape(G, SUBLANES, W)
    sub = lax.broadcasted_iota(I32, A.shape, 1)
    for d in (1, 2, 4):
        keep = sub >= d
        a_sh = jnp.where(keep, pltpu.roll(A, d, 1), 1.0)
        b_sh = jnp.where(keep, pltpu.roll(Bv, d, 1), 0.0)
        Bv = A * b_sh + Bv
        A = A * a_sh
    a_scr[...] = A
    b_scr[...] = Bv

    def body(g, hprev):
        hg = b_scr[g] + a_scr[g] * hprev
        hs_scr[g] = hg
        return hg[SUBLANES - 1:, :]

    carry_scr[...] = lax.fori_loop(0, G, body, carry_scr[...], unroll=8)
    y = (hs_scr[...].reshape(tt, W) * gate).astype(BF16)
    o_ref[0] = x + _dot(y, wout_ref[...])


def _lru_mixer(h, nw, w_in, conv_w, conv_b, w_a, b_a, w_x, b_x, lam, w_out):
    B, S, D = h.shape
    W = w_out.shape[0]
    tt = min(TT_LRU, S)
    hd = W // LRU_HEADS
    wax = jnp.concatenate([w_a, w_x], axis=2).astype(BF16)
    G = tt // SUBLANES
    return pl.pallas_call(
        _lru_kernel,
        out_shape=jax.ShapeDtypeStruct((B, S, D), F32),
        grid=(B, S // tt),
        in_specs=[
            pl.BlockSpec((1, tt, D), lambda b, t: (b, t, 0)),
            _const_spec((1, D)),
            _const_spec((D, 2 * W)),
            _const_spec((CONV_WIDTH, W)),
            _const_spec((1, W)),
            _const_spec((LRU_HEADS, hd, 2 * hd)),
            _const_spec((1, W)),
            _const_spec((1, W)),
            _const_spec((1, W)),
            _const_spec((W, D)),
        ],
        out_specs=pl.BlockSpec((1, tt, D), lambda b, t: (b, t, 0)),
        scratch_shapes=[
            pltpu.VMEM((SUBLANES, W), F32), pltpu.VMEM((1, W), F32),
            pltpu.VMEM((G, SUBLANES, W), F32), pltpu.VMEM((G, SUBLANES, W), F32),
            pltpu.VMEM((G, SUBLANES, W), F32),
        ],
        compiler_params=_params("parallel", "arbitrary"),
        name="rglru_mixer",
    )(h, nw.reshape(1, D), w_in.astype(BF16), conv_w, conv_b.reshape(1, W), wax, b_a.reshape(1, W),
      b_x.reshape(1, W), lam.reshape(1, W), w_out.astype(BF16))


def _pool_kernel(h_ref, halo_ref, nw_ref, wg_ref, sc_ref, o_ref, *, blocks_per_seq):
    tm, D = h_ref.shape
    gsz = D // len(POOL_WINDOWS)
    x = h_ref[...]
    first = pl.program_id(0) % blocks_per_seq == 0
    xn = _rmsnorm(x, nw_ref[...])
    hn_halo = _rmsnorm(halo_ref[...], nw_ref[...])
    hn_halo = jnp.where(first, 0.0, hn_halo)
    e = jnp.concatenate([hn_halo, xn], axis=0)
    t = (pl.program_id(0) % blocks_per_seq) * tm + lax.broadcasted_iota(I32, (tm, 1), 0)
    outs = []
    for g, w in enumerate(POOL_WINDOWS):
        cur = e[:, g * gsz:(g + 1) * gsz]
        base = 0
        d = 1
        while d < w:
            cur = cur[d:, :] + cur[:-d, :]
            base += d
            d *= 2
        win = cur[POOL_HALO - base:POOL_HALO - base + tm, :]
        count = jnp.minimum(t + 1, w).astype(F32)
        pooled = (win / count - xn[:, g * gsz:(g + 1) * gsz]).astype(BF16)
        outs.append(_dot(pooled, wg_ref[g]))
    o_ref[...] = x + jnp.concatenate(outs, axis=1) * sc_ref[...]


def _pool_mixer(h, nw, w_grp, scale, S):
    T, D = h.shape
    tm = min(TM_POOL, S)
    bps = S // tm
    hb = tm // POOL_HALO
    ng, gsz = w_grp.shape[0], w_grp.shape[1]
    return pl.pallas_call(
        functools.partial(_pool_kernel, blocks_per_seq=bps),
        out_shape=jax.ShapeDtypeStruct((T, D), F32),
        grid=(T // tm,),
        in_specs=[
            pl.BlockSpec((tm, D), lambda i: (i, 0)),
            pl.BlockSpec((POOL_HALO, D), lambda i: (jnp.maximum(i * hb - 1, 0), 0)),
            _const_spec((1, D)),
            _const_spec((ng, gsz, gsz)),
            _const_spec((1, D)),
        ],
        out_specs=pl.BlockSpec((tm, D), lambda i: (i, 0)),
        compiler_params=_params("parallel"),
        name="pool_mixer",
    )(h, h, nw.reshape(1, D), w_grp.astype(BF16), scale.reshape(1, D))


def _gmlp_kernel(h_ref, nw_ref, win_ref, lg_ref, lb_ref, ws_ref, bs_ref, wout_ref, o_ref):
    tm, D = h_ref.shape
    W = wout_ref.shape[0]
    gw = W // GMLP_GROUPS
    x = h_ref[...]
    xn = _rmsnorm(x, nw_ref[...]).astype(BF16)
    z = jax.nn.gelu(_dot(xn, win_ref[...]))
    u = z[:, :W]
    v = z[:, W:]
    mu = jnp.mean(v, axis=-1, keepdims=True)
    vc = v - mu
    var = jnp.mean(vc * vc, axis=-1, keepdims=True)
    vb = (vc * lax.rsqrt(var + LN_EPS) * lg_ref[...] + lb_ref[...]).astype(BF16)
    row = lax.broadcasted_iota(I32, (GMLP_CHUNK, GMLP_CHUNK), 0)
    col = lax.broadcasted_iota(I32, (GMLP_CHUNK, GMLP_CHUNK), 1)
    bs = bs_ref[...]
    ws = [jnp.where(col <= row, ws_ref[g], 0.0).astype(BF16) for g in range(GMLP_GROUPS)]
    rows = []
    for c in range(tm // GMLP_CHUNK):
        cols = []
        for g in range(GMLP_GROUPS):
            blk = vb[c * GMLP_CHUNK:(c + 1) * GMLP_CHUNK, g * gw:(g + 1) * gw]
            cols.append(_dot(ws[g], blk) + bs[:, g:g + 1])
        rows.append(jnp.concatenate(cols, axis=1))
    v2 = jnp.concatenate(rows, axis=0)
    o_ref[...] = x + _dot((u * v2).astype(BF16), wout_ref[...])


def _gmlp_mixer(h, nw, w_in, ln_g, ln_b, w_s, b_s, w_out):
    T, D = h.shape
    W = w_out.shape[0]
    tm = min(TM_GMLP, T)
    return pl.pallas_call(
        _gmlp_kernel,
        out_shape=jax.ShapeDtypeStruct((T, D), F32),
        grid=(T // tm,),
        in_specs=[
            pl.BlockSpec((tm, D), lambda i: (i, 0)),
            _const_spec((1, D)),
            _const_spec((D, 2 * W)),
            _const_spec((1, W)),
            _const_spec((1, W)),
            _const_spec((GMLP_GROUPS, GMLP_CHUNK, GMLP_CHUNK)),
            _const_spec((GMLP_CHUNK, GMLP_GROUPS)),
            _const_spec((W, D)),
        ],
        out_specs=pl.BlockSpec((tm, D), lambda i: (i, 0)),
        compiler_params=_params("parallel"),
        name="gmlp_mixer",
    )(h, nw.reshape(1, D), w_in.astype(BF16), ln_g.reshape(1, W), ln_b.reshape(1, W), w_s, b_s.T,
      w_out.astype(BF16))


def _router_kernel(h_ref, nw_ref, whi_ref, wlo_ref, meta_ref, gate_ref, cnt_ref, cnt_scr):
    tm = h_ref.shape[0]

    @pl.when(pl.program_id(0) == 0)
    def _():
        cnt_scr[...] = jnp.zeros(cnt_scr.shape, F32)

    xn = _rmsnorm(h_ref[...], nw_ref[...])
    x_hi = xn.astype(BF16)
    x_lo = (xn - x_hi.astype(F32)).astype(BF16)
    nt = (((1,), (1,)), ((), ()))
    logits = (lax.dot_general(whi_ref[...], x_hi, nt, preferred_element_type=F32)
              + lax.dot_general(whi_ref[...], x_lo, nt, preferred_element_type=F32)
              + lax.dot_general(wlo_ref[...], x_hi, nt, preferred_element_type=F32))
    eid = lax.broadcasted_iota(I32, logits.shape, 0).astype(F32)
    m1 = jnp.max(logits, axis=0, keepdims=True)
    e1 = jnp.min(jnp.where(logits == m1, eid, float(N_EXPERTS)), axis=0, keepdims=True)
    rest = jnp.where(eid == e1, -jnp.inf, logits)
    m2 = jnp.max(rest, axis=0, keepdims=True)
    e2 = jnp.min(jnp.where(rest == m2, eid, float(N_EXPERTS)), axis=0, keepdims=True)
    ex = jnp.exp(m2 - m1)
    g1 = 1.0 / (1.0 + ex)
    g2 = ex / (1.0 + ex)
    hit1 = eid == e1
    hit2 = eid == e2
    onehot = jnp.where(hit1 | hit2, 1.0, 0.0)
    row = lax.broadcasted_iota(I32, (tm, tm), 0)
    col = lax.broadcasted_iota(I32, (tm, tm), 1)
    earlier = jnp.where(row < col, 1.0, 0.0).astype(BF16)
    before = _dot(onehot.astype(BF16), earlier) + cnt_scr[...]
    r1 = jnp.sum(jnp.where(hit1, before, 0.0), axis=0, keepdims=True)
    r2 = jnp.sum(jnp.where(hit2, before, 0.0), axis=0, keepdims=True)
    cnt_scr[...] += jnp.sum(onehot, axis=1, keepdims=True)
    zero = jnp.zeros((SUBLANES - 4, tm), F32)
    meta_ref[...] = jnp.concatenate([e1, e2, r1, r2, zero], axis=0).astype(I32)
    gate_ref[...] = jnp.concatenate([g1, g2, zero, 0.0 * g1, 0.0 * g2], axis=0)
    cnt_ref[...] = jnp.broadcast_to(cnt_scr[...], cnt_ref.shape)


def _router(h, nw, w_router):
    T, D = h.shape
    assert N_EXPERTS == SUBLANES
    tm = min(TM_ROUTE, T)
    tok_spec = pl.BlockSpec((SUBLANES, tm), lambda i: (0, i))
    w_hi = w_router.T.astype(BF16)
    w_lo = (w_router.T - w_hi.astype(F32)).astype(BF16)
    return pl.pallas_call(
        _router_kernel,
        out_shape=(jax.ShapeDtypeStruct((SUBLANES, T), I32), jax.ShapeDtypeStruct((SUBLANES, T), F32),
                   jax.ShapeDtypeStruct((N_EXPERTS, LANES), F32)),
        grid=(T // tm,),
        in_specs=[pl.BlockSpec((tm, D), lambda i: (i, 0)), _const_spec((1, D)), _const_spec((N_EXPERTS, D)),
                  _const_spec((N_EXPERTS, D))],
        out_specs=(tok_spec, tok_spec, _const_spec((N_EXPERTS, LANES))),
        scratch_shapes=[pltpu.VMEM((N_EXPERTS, 1), F32)],
        compiler_params=_params("arbitrary"),
        name="moe_router",
    )(h, nw.reshape(1, D), w_hi, w_lo)


ZERO_ROWS = 512


def _dispatch_kernel(zs_ref, zl_ref, d1_ref, d2_ref, h_ref, xs_ref, rows_scr, zero_scr, sem, zsem):
    tm = h_ref.shape[0]
    _tile_to_rows(rows_scr, h_ref[...], tm)

    def copies(r):
        src = rows_scr.at[pl.ds(pl.multiple_of(r * SUBLANES, SUBLANES), SUBLANES), :]
        return [pltpu.make_async_copy(
            src, xs_ref.at[pl.ds(pl.multiple_of(d[0, 0, r] * SUBLANES, SUBLANES), SUBLANES), :], sem)
            for d in (d1_ref, d2_ref)]

    def start(r, c):
        for prio, cp in enumerate(copies(r)):
            cp.start(priority=prio)
        return c

    def wait(r, c):
        for cp in copies(r):
            cp.wait()
        return c

    def zero_copies(fn):
        def zero_dma(off, size, pred):
            cp = pltpu.make_async_copy(
                zero_scr.at[pl.ds(0, size * SUBLANES), :],
                xs_ref.at[pl.ds(pl.multiple_of(off * SUBLANES, SUBLANES), size * SUBLANES), :], zsem)
            pl.when(pred)(functools.partial(fn, cp))

        for k in range(N_EXPERTS + 1):
            tail = k == N_EXPERTS
            max_rows = 2 * ZERO_ROWS * (N_EXPERTS if tail else 1)
            n = zl_ref[k]
            whole = n // ZERO_ROWS
            for c in range(max_rows // ZERO_ROWS):
                zero_dma(zs_ref[k] + c * ZERO_ROWS, ZERO_ROWS, c < whole)
            base = zs_ref[k] + whole * ZERO_ROWS
            rest = n % ZERO_ROWS
            size = ZERO_ROWS // 2
            while size >= 1:
                zero_dma(base + (rest // (2 * size)) * (2 * size), size, (rest // size) % 2 == 1)
                size //= 2

    first = pl.program_id(0) == 0

    @pl.when(first)
    def _():
        zero_scr[...] = jnp.zeros(zero_scr.shape, F32)
        zero_copies(lambda cp: cp.start())

    lax.fori_loop(0, tm, start, 0, unroll=4)
    lax.fori_loop(0, tm, wait, 0, unroll=4)

    @pl.when(first)
    def _():
        zero_copies(lambda cp: cp.wait())


def _dispatch(h, d1, d2, zero_start, zero_len, P):
    T, D = h.shape
    tm = min(TM_DISPATCH, T)
    nb = T // tm
    smem = functools.partial(pl.BlockSpec, memory_space=pltpu.SMEM)
    grid_spec = pltpu.PrefetchScalarGridSpec(
        num_scalar_prefetch=2,
        grid=(nb,),
        in_specs=[
            smem((1, 1, tm), lambda i, zs, zl: (i, 0, 0)),
            smem((1, 1, tm), lambda i, zs, zl: (i, 0, 0)),
            pl.BlockSpec((tm, D), lambda i, zs, zl: (i, 0)),
        ],
        out_specs=pl.BlockSpec(memory_space=pl.ANY),
        scratch_shapes=[pltpu.VMEM((tm * SUBLANES, LANES), F32), pltpu.VMEM((ZERO_ROWS * SUBLANES, LANES), F32),
                        pltpu.SemaphoreType.DMA, pltpu.SemaphoreType.DMA],
    )
    return pl.pallas_call(
        _dispatch_kernel,
        out_shape=jax.ShapeDtypeStruct((P * SUBLANES, LANES), F32),
        grid_spec=grid_spec,
        compiler_params=_params("arbitrary"),
        name="moe_dispatch",
    )(zero_start, zero_len, d1.reshape(nb, 1, tm), d2.reshape(nb, 1, tm), h)


def _combine_kernel(d1_ref, d2_ref, n1_ref, n2_ref, h_ref, gate_ref, ys_ref, *rest, final_norm):
    if final_norm:
        fw_ref, o_ref, buf1, buf2, sem = rest
    else:
        o_ref, buf1, buf2, sem = rest
    tm = h_ref.shape[0]
    step = pl.program_id(0)
    slot = step % 2

    def copies(srcs, buf, r):
        dst = pl.ds(pl.multiple_of(r * SUBLANES, SUBLANES), SUBLANES)
        return [pltpu.make_async_copy(
            ys_ref.at[pl.ds(pl.multiple_of(d[0, 0, r] * SUBLANES, SUBLANES), SUBLANES), :],
            b.at[buf, dst, :], sem.at[buf]) for d, b in zip(srcs, (buf1, buf2))]

    def issue(srcs, buf):
        def start(r, c):
            for prio, cp in enumerate(copies(srcs, buf, r)):
                cp.start(priority=prio)
            return c

        lax.fori_loop(0, tm, start, 0, unroll=4)

    @pl.when(step == 0)
    def _():
        issue((d1_ref, d2_ref), slot)

    @pl.when(step + 1 < pl.num_programs(0))
    def _():
        issue((n1_ref, n2_ref), 1 - slot)

    def wait(r, c):
        for cp in copies((d1_ref, d2_ref), slot, r):
            cp.wait()
        return c

    lax.fori_loop(0, tm, wait, 0, unroll=4)
    gates = gate_ref[...]
    out = (h_ref[...] + gates[:, 0:1] * _rows_to_tile(buf1.at[slot], tm)
           + gates[:, 1:2] * _rows_to_tile(buf2.at[slot], tm))
    if final_norm:
        out = _rmsnorm(out, fw_ref[...])
    o_ref[...] = out


def _combine(h, gates, ys, d1, d2, final_w=None):
    T, D = h.shape
    tm = min(TM_COMBINE, T)
    nb = T // tm
    smem = functools.partial(pl.BlockSpec, memory_space=pltpu.SMEM)
    in_specs = [
        smem((1, 1, tm), lambda i: (i, 0, 0)),
        smem((1, 1, tm), lambda i: (i, 0, 0)),
        smem((1, 1, tm), lambda i: (jnp.minimum(i + 1, nb - 1), 0, 0)),
        smem((1, 1, tm), lambda i: (jnp.minimum(i + 1, nb - 1), 0, 0)),
        pl.BlockSpec((tm, D), lambda i: (i, 0)),
        pl.BlockSpec((tm, TOP_K), lambda i: (i, 0)),
        pl.BlockSpec(memory_space=pl.ANY),
    ]
    d1 = d1.reshape(nb, 1, tm)
    d2 = d2.reshape(nb, 1, tm)
    args = [d1, d2, d1, d2, h, gates, ys]
    if final_w is not None:
        in_specs.append(_const_spec((1, D)))
        args.append(final_w.reshape(1, D))
    buf = pltpu.VMEM((2, tm * SUBLANES, LANES), F32)
    return pl.pallas_call(
        functools.partial(_combine_kernel, final_norm=final_w is not None),
        out_shape=jax.ShapeDtypeStruct((T, D), F32),
        grid=(nb,),
        in_specs=in_specs,
        out_specs=pl.BlockSpec((tm, D), lambda i: (i, 0)),
        scratch_shapes=[buf, buf, pltpu.SemaphoreType.DMA((2,))],
        compiler_params=_params("arbitrary"),
        name="moe_combine",
    )(*args)


def _moe_layer(h, nw, w_router, wg, wu, wd, layer, final_w=None):
    T, D = h.shape
    meta, gates, counts = _router(h, nw, w_router)
    tm = min(TM_MOE, T)
    sizes = counts[:, 0].astype(I32)
    padded = (sizes + tm - 1) // tm * tm
    pad_end = jnp.cumsum(padded)
    pad_start = pad_end - padded
    d1 = pad_start[meta[0]] + meta[2]
    d2 = pad_start[meta[1]] + meta[3]
    n_tiles = (T * TOP_K + N_EXPERTS * (tm - 1)) // tm
    tile_start = jnp.arange(n_tiles, dtype=I32) * tm
    tile_expert = jnp.minimum(jnp.sum(pad_end[None, :] <= tile_start[:, None], axis=1), N_EXPERTS - 1)
    n_valid = (pad_end[-1] // tm).reshape(1)
    assert tm <= 2 * ZERO_ROWS
    total = n_tiles * tm
    zero_start = jnp.concatenate([pad_start + sizes, pad_end[-1:]])
    zero_len = jnp.concatenate([padded - sizes, total - pad_end[-1:]])
    xs = _dispatch(h, d1, d2, zero_start, zero_len, total)
    tile_rows = jnp.clip((pad_start + sizes)[tile_expert] - tile_start, 0, tm)
    ys = _ffn_moe(xs, nw, wg, wu, wd, layer, tile_expert.astype(I32), n_valid.astype(I32),
                  tile_rows.astype(I32), tm)
    return _combine(h, gates[:TOP_K].T, ys, d1, d2, final_w)


def kernel(x, positions, norm_mix, norm_ffn, norm_final, mla_w_in, mla_q_norm, mla_kv_norm, mla_w_uq,
           mla_w_ukv, mla_w_o, lru_w_in, lru_conv_w, lru_conv_b, lru_w_a, lru_b_a, lru_w_x, lru_b_x,
           lru_lam, lru_w_out, pool_w_grp, pool_scale, gmlp_w_in, gmlp_ln_g, gmlp_ln_b, gmlp_w_s,
           gmlp_b_s, gmlp_w_out, ffn_w_gate, ffn_w_up, ffn_w_down, moe_w_router, moe_w_gate, moe_w_up,
           moe_w_down):
    B, S, D = x.shape
    T = B * S
    depth = norm_mix.shape[0]
    h = x.reshape(T, D)
    for i in range(depth):
        m, j = i % 4, i // 4
        if m == 0:
            q, k, v = _mla_proj(h, positions.reshape(1, T), norm_mix[i], mla_w_in[j], mla_q_norm[j],
                                mla_kv_norm[j], mla_w_uq[j], mla_w_ukv[j], B, S)
            o = _attention(q, k, v)
            h = _matmul_residual(o.reshape(T, -1), mla_w_o[j], h)
        elif m == 1:
            h = _lru_mixer(h.reshape(B, S, D), norm_mix[i], lru_w_in[j], lru_conv_w[j], lru_conv_b[j],
                           lru_w_a[j], lru_b_a[j], lru_w_x[j], lru_b_x[j], lru_lam[j],
                           lru_w_out[j]).reshape(T, D)
        elif m == 2:
            h = _pool_mixer(h, norm_mix[i], pool_w_grp[j], pool_scale[j], S)
        else:
            h = _gmlp_mixer(h, norm_mix[i], gmlp_w_in[j], gmlp_ln_g[j], gmlp_ln_b[j], gmlp_w_s[j],
                            gmlp_b_s[j], gmlp_w_out[j])
        kk = i // 2
        if i % 2 == 0:
            h = _ffn_dense(h, norm_ffn[i], ffn_w_gate, ffn_w_up, ffn_w_down, kk)
        else:
            last = i == depth - 1
            h = _moe_layer(h, norm_ffn[i], moe_w_router[kk], moe_w_gate, moe_w_up, moe_w_down, kk,
                           norm_final if last else None)
    if depth % 2 == 1:
        raise NotImplementedError("final RMSNorm is fused into the last MoE combine")
    return h.reshape(B, S, D)
```
